```python
import jax
import jax.numpy as jnp
from jax import lax
import numpy as np

D_MODEL = 4096
BATCH = 2
SEQ = 8192
DEPTH = 1

CHUNK = 64

SGU_BLOCK = 128
SGU_WIDTH = D_MODEL // 2
SGU_GROUPS = 8
SGU_GROUP_DIM = SGU_WIDTH // SGU_GROUPS

SB_HEAD_DIM = 128
SB_WIDTH = D_MODEL // 2
SB_HEADS = SB_WIDTH // SB_HEAD_DIM
Q_BLOCK = 128

N_BRANCH = 2
IN_SPLITS = (SGU_WIDTH, 2 * SGU_WIDTH, 2 * SGU_WIDTH + SB_WIDTH,
             2 * SGU_WIDTH + 2 * SB_WIDTH, 2 * SGU_WIDTH + 3 * SB_WIDTH)
IN_COLS = 2 * SGU_WIDTH + 3 * SB_WIDTH + N_BRANCH * D_MODEL

N_GROUPS_MOE = 4
EXPERTS_PER_GROUP = 8
N_EXPERTS = N_GROUPS_MOE * EXPERTS_PER_GROUP
TOP_K_IN_GROUP = 2
EXPERT_HIDDEN = D_MODEL // 4
MOE_BLOCK = 128

EPS = 1e-6

kernel_name = "hybrid_sgu_stickbreak_hiermoe_block"


def rms_norm(x, g):
    xf = x.astype(jnp.float32)
    y = xf * lax.rsqrt(jnp.mean(xf * xf, axis=-1, keepdims=True) + EPS)
    return (y * g.astype(jnp.float32)).astype(x.dtype)


def spatial_gating(u, v, g_sgu, w_sgu, b_sgu):
    B, S, _ = u.shape
    nblk = S // SGU_BLOCK
    v = rms_norm(v, g_sgu)
    chunk_id = jnp.arange(SGU_BLOCK) // CHUNK
    causal = chunk_id[None, :] <= chunk_id[:, None]
    w = jnp.where(causal[None], w_sgu, 0).astype(v.dtype)
    vb = v.reshape(B, nblk, SGU_BLOCK, SGU_GROUPS, SGU_GROUP_DIM)
    mixed = jnp.einsum('gts,bnsgc->bntgc', w, vb) + b_sgu.T[None, None, :, :, None].astype(v.dtype)
    return u * mixed.reshape(B, S, SGU_WIDTH)


def stick_breaking_attention(q, k, v):
    B, S, H, Dh = q.shape
    nq = S // Q_BLOCK
    scale = Dh ** -0.5
    kf = k.astype(jnp.float32)
    vf = v.astype(jnp.float32)
    key_pos = jnp.arange(S)
    q_blocks = q.reshape(B, nq, Q_BLOCK, H, Dh).transpose(1, 0, 2, 3, 4)

    def one_block(args):
        qb, n = args
        z = jnp.einsum('bthd,bshd->bhts', qb.astype(jnp.float32), kf) * scale
        q_pos = n * Q_BLOCK + jnp.arange(Q_BLOCK)
        mask = key_pos[None, :] < q_pos[:, None]
        log_1m_beta = jnp.where(mask, jax.nn.log_sigmoid(-z), 0.0)
        between = lax.cumsum(log_1m_beta, axis=3, reverse=True) - log_1m_beta
        a = jnp.where(mask, jnp.exp(jax.nn.log_sigmoid(z) + between), 0.0)
        return jnp.einsum('bhts,bshd->bthd', a, vf)

    o = lax.map(one_block, (q_blocks, jnp.arange(nq)))
    return o.transpose(1, 0, 2, 3, 4).reshape(B, S, H * Dh).astype(q.dtype)


def hybrid_mixer(xn, w_in, g_sgu, w_sgu, b_sgu, b_gate, w_proj_a, w_proj_b, w_out):
    B, S, _ = xn.shape
    proj = xn @ w_in
    u, v, q, k, v_sb, gate_pre = jnp.split(proj, IN_SPLITS, axis=-1)
    y_a = spatial_gating(jax.nn.gelu(u), jax.nn.gelu(v), g_sgu, w_sgu, b_sgu) @ w_proj_a
    q = q.reshape(B, S, SB_HEADS, SB_HEAD_DIM)
    k = k.reshape(B, S, SB_HEADS, SB_HEAD_DIM)
    v_sb = v_sb.reshape(B, S, SB_HEADS, SB_HEAD_DIM)
    y_b = stick_breaking_attention(q, k, v_sb) @ w_proj_b
    gates = jax.nn.sigmoid(gate_pre.reshape(B, S, N_BRANCH, D_MODEL) + b_gate)
    merged = gates[:, :, 0] * y_a + gates[:, :, 1] * y_b
    return merged @ w_out


def hierarchical_moe(xn, w_group, b_group, w_router, b_router, w_gate, w_up, w_down):
    B, S, D = xn.shape
    T = B * S
    xf = xn.reshape(T, D)
    group_logits = (xf @ w_group).astype(jnp.float32) + b_group.astype(jnp.float32)
    group_prob = jax.nn.softmax(group_logits, axis=-1)
    grp = jnp.argmax(group_logits, axis=-1)
    p_grp = jnp.take_along_axis(group_prob, grp[:, None], axis=-1)
    expert_logits = ((xf @ w_router).astype(jnp.float32) + b_router.astype(jnp.float32))
    expert_logits = expert_logits.reshape(T, N_GROUPS_MOE, EXPERTS_PER_GROUP)
    local_logits = jnp.take_along_axis(expert_logits, grp[:, None, None], axis=1)[:, 0]
    top_val, top_idx = lax.top_k(local_logits, TOP_K_IN_GROUP)
    w_assign = jax.nn.softmax(top_val, axis=-1) * p_grp
    expert_id = grp[:, None] * EXPERTS_PER_GROUP + top_idx

    A = T * TOP_K_IN_GROUP
    e_flat = expert_id.reshape(A)
    tok_flat = jnp.repeat(jnp.arange(T, dtype=jnp.int32), TOP_K_IN_GROUP)
    w_flat = w_assign.reshape(A)
    order = jnp.argsort(e_flat)
    e_sorted = e_flat[order]
    tok_sorted = tok_flat[order]
    w_sorted = w_flat[order]
    counts = jnp.zeros((N_EXPERTS,), jnp.int32).at[e_flat].add(1)
    starts = jnp.cumsum(counts) - counts
    padded = (counts + MOE_BLOCK - 1) // MOE_BLOCK * MOE_BLOCK
    pad_end = jnp.cumsum(padded)
    pad_start = pad_end - padded
    dest = pad_start[e_sorted] + (jnp.arange(A, dtype=jnp.int32) - starts[e_sorted])
    n_blocks = A // MOE_BLOCK + N_EXPERTS
    rows = n_blocks * MOE_BLOCK
    row_tok = jnp.full((rows,), T, jnp.int32).at[dest].set(tok_sorted)
    row_w = jnp.zeros((rows,), jnp.float32).at[dest].set(w_sorted)
    block_start = jnp.arange(n_blocks, dtype=jnp.int32) * MOE_BLOCK
    block_expert = jnp.minimum(jnp.searchsorted(pad_end, block_start, side='right'), N_EXPERTS - 1)
    x_pad = jnp.concatenate([xf, jnp.zeros((1, D), xf.dtype)], axis=0)

    def expert_block(out, blk):
        e, tok, wt = blk
        xb = x_pad[tok]
        h = jax.nn.silu(xb @ w_gate[e]) * (xb @ w_up[e])
        yb = (h @ w_down[e]) * wt[:, None].astype(h.dtype)
        return out.at[tok].add(yb.astype(out.dtype)), None

    out, _ = lax.scan(expert_block, jnp.zeros((T + 1, D), xf.dtype),
                      (block_expert, row_tok.reshape(n_blocks, MOE_BLOCK),
                       row_w.reshape(n_blocks, MOE_BLOCK)))
    return out[:T].reshape(B, S, D)


def setup_inputs(seed: int = 0) -> dict:
    key = jax.random.key(seed)
    ks = jax.random.split(key, 20)
    L = DEPTH
    f32 = jnp.float32

    def dense(k, shape, fan_in):
        return jax.random.normal(k, shape, f32) * (fan_in ** -0.5)

    def gain(k, shape):
        return 1.0 + 0.02 * jax.random.normal(k, shape, f32)

    return {
        "x": jax.random.normal(ks[0], (BATCH, SEQ, D_MODEL), f32),
        "g_mix": gain(ks[1], (L, D_MODEL)),
        "w_in": dense(ks[2], (L, D_MODEL, IN_COLS), D_MODEL),
        "g_sgu": gain(ks[3], (L, SGU_WIDTH)),
        "w_sgu": dense(ks[4], (L, SGU_GROUPS, SGU_BLOCK, SGU_BLOCK), SGU_BLOCK),
        "b_sgu": gain(ks[5], (L, SGU_GROUPS, SGU_BLOCK)),
        "b_gate": 0.1 * jax.random.normal(ks[6], (L, N_BRANCH, D_MODEL), f32),
        "w_proj_a": dense(ks[7], (L, SGU_WIDTH, D_MODEL), SGU_WIDTH),
        "w_proj_b": dense(ks[8], (L, SB_WIDTH, D_MODEL), SB_WIDTH),
        "w_out": dense(ks[9], (L, D_MODEL, D_MODEL), D_MODEL),
        "g_ffn": gain(ks[10], (L, D_MODEL)),
        "w_group": dense(ks[11], (L, D_MODEL, N_GROUPS_MOE), D_MODEL),
        "b_group": 0.01 * jax.random.normal(ks[12], (L, N_GROUPS_MOE), f32),
        "w_router": dense(ks[13], (L, D_MODEL, N_EXPERTS), D_MODEL),
        "b_router": 0.01 * jax.random.normal(ks[14], (L, N_EXPERTS), f32),
        "w_gate": dense(ks[15], (L, N_EXPERTS, D_MODEL, EXPERT_HIDDEN), D_MODEL),
        "w_up": dense(ks[16], (L, N_EXPERTS, D_MODEL, EXPERT_HIDDEN), D_MODEL),
        "w_down": dense(ks[17], (L, N_EXPERTS, EXPERT_HIDDEN, D_MODEL), EXPERT_HIDDEN),
        "g_final": gain(ks[18], (D_MODEL,)),
    }


def reference(x, g_mix, w_in, g_sgu, w_sgu, b_sgu, b_gate, w_proj_a, w_proj_b, w_out,
              g_ffn, w_group, b_group, w_router, b_router, w_gate, w_up, w_down, g_final):
    for l in range(DEPTH):
        xn = rms_norm(x, g_mix[l])
        x = x + hybrid_mixer(xn, w_in[l], g_sgu[l], w_sgu[l], b_sgu[l], b_gate[l],
                             w_proj_a[l], w_proj_b[l], w_out[l])
        xn = rms_norm(x, g_ffn[l])
        x = x + hierarchical_moe(xn, w_group[l], b_group[l], w_router[l], b_router[l],
                                 w_gate[l], w_up[l], w_down[l])
    return rms_norm(x, g_final)
```

```python
import functools

import jax
import jax.numpy as jnp
from jax import lax
from jax.experimental import pallas as pl
from jax.experimental.pallas import tpu as pltpu

F32 = jnp.float32
BF16 = jnp.bfloat16

EPS = 1e-6
CHUNK = 64
SGU_BLOCK = 128
SGU_GROUPS = 8
HEAD_DIM = 128
N_GROUPS_MOE = 4
EXPERTS_PER_GROUP = 8
TOP_K = 2
ROUTE_LANES = 128

SKIP_BELOW = 120.0

VMEM_LIMIT = 56 * 2**20


def _tile(n, pref):
    t = min(n, pref)
    assert n % t == 0, (n, pref)
    return t


def _params(sem):
    return pltpu.CompilerParams(dimension_semantics=sem, vmem_limit_bytes=VMEM_LIMIT)


def _sigmoid(x):
    return 1.0 / (1.0 + jnp.exp(-x))


def _gelu_tanh(x):
    c = 0.7978845608028654
    return x * (0.5 * (1.0 + jnp.tanh(c * (x + 0.044715 * (x * x * x)))))


def _rms(x, g):
    return x * lax.rsqrt(jnp.mean(x * x, axis=-1, keepdims=True) + EPS) * g


def _norm_kernel(x_ref, g_ref, o_ref):
    o_ref[...] = _rms(x_ref[...], g_ref[...]).astype(o_ref.dtype)


def _norm_bf16(x, g):
    T, D = x.shape
    tm = _tile(T, 512)
    return pl.pallas_call(
        _norm_kernel,
        grid=(T // tm,),
        in_specs=[pl.BlockSpec((tm, D), lambda i: (i, 0)),
                  pl.BlockSpec((1, D), lambda i: (0, 0))],
        out_specs=pl.BlockSpec((tm, D), lambda i: (i, 0)),
        out_shape=jax.ShapeDtypeStruct((T, D), BF16),
        compiler_params=_params(("parallel",)),
        name="norm_bf16",
    )(x, g.reshape(1, D))


def _proj_kernel(x_ref, w_ref, b_ref, o_ref, *, n_gelu, n_q_end, n_plain_end, q_scale):
    j = pl.program_id(1)
    acc = jnp.dot(x_ref[...], w_ref[...], preferred_element_type=F32)

    @pl.when(j < n_gelu)
    def _():
        o_ref[...] = _gelu_tanh(acc).astype(o_ref.dtype)

    @pl.when((j >= n_gelu) & (j < n_q_end))
    def _():
        o_ref[...] = (acc * q_scale).astype(o_ref.dtype)

    @pl.when((j >= n_q_end) & (j < n_plain_end))
    def _():
        o_ref[...] = acc.astype(o_ref.dtype)

    @pl.when(j >= n_plain_end)
    def _():
        o_ref[...] = _sigmoid(acc + b_ref[...]).astype(o_ref.dtype)


def _in_proj(xn, w_in, b_gate, W):
    T, D = xn.shape
    N = w_in.shape[1]
    tm = _tile(T, 1024)
    tn = _tile(W, 1024)
    n_gelu = 2 * W // tn
    n_q_end = 3 * W // tn
    n_plain_end = 5 * W // tn
    kern = functools.partial(_proj_kernel, n_gelu=n_gelu, n_q_end=n_q_end,
                             n_plain_end=n_plain_end, q_scale=HEAD_DIM ** -0.5)
    return pl.pallas_call(
        kern,
        grid=(T // tm, N // tn),
        in_specs=[pl.BlockSpec((tm, D), lambda i, j: (i, 0)),
                  pl.BlockSpec((D, tn), lambda i, j: (0, j)),
                  pl.BlockSpec((1, tn), lambda i, j: (0, jnp.maximum(j - n_plain_end, 0)))],
        out_specs=pl.BlockSpec((tm, tn), lambda i, j: (i, j)),
        out_shape=jax.ShapeDtypeStruct((T, N), BF16),
        compiler_params=_params(("parallel", "arbitrary")),
        name="in_proj",
    )(xn, w_in, b_gate.reshape(1, -1))


def _sgu_kernel(u_ref, v_ref, g_ref, w_ref, bt_ref, o_ref, *, n_sub, gd):
    v = v_ref[...].astype(F32)
    vn = _rms(v, g_ref[...]).astype(BF16)
    t_chunk = lax.broadcasted_iota(jnp.int32, (SGU_BLOCK, SGU_BLOCK), 0) // CHUNK
    s_chunk = lax.broadcasted_iota(jnp.int32, (SGU_BLOCK, SGU_BLOCK), 1) // CHUNK
    causal = s_chunk <= t_chunk
    for g in range(SGU_GROUPS):
        w = jnp.where(causal, w_ref[g], 0.0).astype(BF16)
        bias = bt_ref[:, g:g + 1]
        cols = slice(g * gd, (g + 1) * gd)
        for n in range(n_sub):
            rows = slice(n * SGU_BLOCK, (n + 1) * SGU_BLOCK)
            mixed = jnp.dot(w, vn[rows, cols], preferred_element_type=F32) + bias
            o_ref[rows, cols] = (u_ref[rows, cols].astype(F32) * mixed).astype(o_ref.dtype)


def _sgu(proj, g_sgu, w_sgu, b_sgu, W):
    T = proj.shape[0]
    tm = _tile(T, 2 * SGU_BLOCK)
    gd = W // SGU_GROUPS
    kern = functools.partial(_sgu_kernel, n_sub=tm // SGU_BLOCK, gd=gd)
    return pl.pallas_call(
        kern,
        grid=(T // tm,),
        in_specs=[pl.BlockSpec((tm, W), lambda i: (i, 0)),
                  pl.BlockSpec((tm, W), lambda i: (i, 1)),
                  pl.BlockSpec((1, W), lambda i: (0, 0)),
                  pl.BlockSpec((SGU_GROUPS, SGU_BLOCK, SGU_BLOCK), lambda i: (0, 0, 0)),
                  pl.BlockSpec((SGU_BLOCK, SGU_GROUPS), lambda i: (0, 0))],
        out_specs=pl.BlockSpec((tm, W), lambda i: (i, 0)),
        out_shape=jax.ShapeDtypeStruct((T, W), BF16),
        compiler_params=_params(("parallel",)),
        name="sgu",
    )(proj, proj, g_sgu.reshape(1, W), w_sgu, b_sgu.T)


def _attn_kernel(q_ref, k_ref, v_ref, o_ref, *, tq, heads):
    qi = pl.program_id(2)
    jj = lax.broadcasted_iota(jnp.int32, (tq, tq), 0)
    ss = lax.broadcasted_iota(jnp.int32, (tq, tq), 1)
    tri = (jj >= ss).astype(BF16)
    before = ss < jj

    def block(q, kb, h, carry, acc, diagonal):
        cols = slice(h * HEAD_DIM, (h + 1) * HEAD_DIM)
        k = k_ref[pl.ds(pl.multiple_of(kb * tq, tq), tq), cols]
        v = v_ref[pl.ds(pl.multiple_of(kb * tq, tq), tq), cols]
        z = lax.dot_general(q, k, (((1,), (1,)), ((), ())), preferred_element_type=F32)
        lg = -(jnp.maximum(z, 0.0) + jnp.log1p(jnp.exp(-jnp.abs(z))))
        if diagonal:
            lg = jnp.where(before, lg, 0.0)
        hi = lg.astype(BF16)
        lo = (lg - hi.astype(F32)).astype(BF16)
        incl = (jnp.dot(hi, tri, preferred_element_type=F32)
                + jnp.dot(lo, tri, preferred_element_type=F32))
        a = jnp.exp(z + incl + carry)
        if diagonal:
            a = jnp.where(before, a, 0.0)
        acc = acc + jnp.dot(a.astype(BF16), v, preferred_element_type=F32)
        carry = carry + incl[:, 0:1]
        return carry, acc

    for h in range(heads):
        q = q_ref[:, h * HEAD_DIM:(h + 1) * HEAD_DIM]
        carry, acc = block(q, qi, h, jnp.zeros((tq, 1), F32),
                           jnp.zeros((tq, HEAD_DIM), F32), True)

        def cond(state):
            kb, carry, _ = state
            return (kb >= 0) & (jnp.max(carry) > -SKIP_BELOW)

        def body(state, q=q, h=h):
            kb, carry, acc = state
            carry, acc = block(q, kb, h, carry, acc, False)
            return kb - 1, carry, acc

        _, _, acc = lax.while_loop(cond, body, (qi - 1, carry, acc))
        o_ref[:, h * HEAD_DIM:(h + 1) * HEAD_DIM] = acc.astype(o_ref.dtype)


def _attention(proj, B, S, W):
    T = proj.shape[0]
    n_heads = W // HEAD_DIM
    heads = min(4, n_heads)
    assert n_heads % heads == 0
    hw = heads * HEAD_DIM
    tq = _tile(S, 256)
    nq = S // tq
    q0, k0, v0 = 2 * W // hw, 3 * W // hw, 4 * W // hw
    kern = functools.partial(_attn_kernel, tq=tq, heads=heads)
    return pl.pallas_call(
        kern,
        grid=(B, n_heads // heads, nq),
        in_specs=[pl.BlockSpec((tq, hw), lambda b, g, i: (b * nq + i, q0 + g)),
                  pl.BlockSpec((S, hw), lambda b, g, i: (b, k0 + g)),
                  pl.BlockSpec((S, hw), lambda b, g, i: (b, v0 + g))],
        out_specs=pl.BlockSpec((tq, hw), lambda b, g, i: (b * nq + i, g)),
        out_shape=jax.ShapeDtypeStruct((T, W), BF16),
        compiler_params=_params(("parallel", "parallel", "arbitrary")),
        name="stick_breaking",
    )(proj, proj, proj)


def _merge_kernel(a_ref, b_ref, pa_ref, pb_ref, ga_ref, gb_ref, o_ref):
    ya = jnp.dot(a_ref[...], pa_ref[...], preferred_element_type=F32)
    yb = jnp.dot(b_ref[...], pb_ref[...], preferred_element_type=F32)
    o_ref[...] = (ga_ref[...].astype(F32) * ya + gb_ref[...].astype(F32) * yb).astype(o_ref.dtype)


def _merge(a, b, proj, w_pa, w_pb, W, D):
    T = a.shape[0]
    tm = _tile(T, 512)
    tn = _tile(W, 1024)
    ga0 = 5 * W // tn
    gb0 = (5 * W + D) // tn
    return pl.pallas_call(
        _merge_kernel,
        grid=(T // tm, D // tn),
        in_specs=[pl.BlockSpec((tm, W), lambda i, j: (i, 0)),
                  pl.BlockSpec((tm, W), lambda i, j: (i, 0)),
                  pl.BlockSpec((W, tn), lambda i, j: (0, j)),
                  pl.BlockSpec((W, tn), lambda i, j: (0, j)),
                  pl.BlockSpec((tm, tn), lambda i, j: (i, ga0 + j)),
                  pl.BlockSpec((tm, tn), lambda i, j: (i, gb0 + j))],
        out_specs=pl.BlockSpec((tm, tn), lambda i, j: (i, j)),
        out_shape=jax.ShapeDtypeStruct((T, D), BF16),
        compiler_params=_params(("parallel", "arbitrary")),
        name="merge",
    )(a, b, w_pa, w_pb, proj, proj)


def _out_kernel(m_ref, w_ref, x_ref, o_ref):
    o_ref[...] = x_ref[...] + jnp.dot(m_ref[...], w_ref[...], preferred_element_type=F32)


def _out_proj(merged, w_out, x):
    T, D = x.shape
    tm = _tile(T, 512)
    tn = _tile(D, 1024)
    return pl.pallas_call(
        _out_kernel,
        grid=(T // tm, D // tn),
        in_specs=[pl.BlockSpec((tm, D), lambda i, j: (i, 0)),
                  pl.BlockSpec((D, tn), lambda i, j: (0, j)),
                  pl.BlockSpec((tm, tn), lambda i, j: (i, j))],
        out_specs=pl.BlockSpec((tm, tn), lambda i, j: (i, j)),
        out_shape=jax.ShapeDtypeStruct((T, D), F32),
        compiler_params=_params(("parallel", "arbitrary")),
        name="out_proj",
    )(merged, w_out, x)


def _route_kernel(x_ref, g_ref, wr_ref, br_ref, xn_ref, e_ref, p_ref):
    xn = _rms(x_ref[...], g_ref[...])
    xn_ref[...] = xn
    logits = jnp.dot(xn, wr_ref[...], precision=lax.Precision.HIGHEST,
                     preferred_element_type=F32) + br_ref[...]
    lane = lax.broadcasted_iota(jnp.int32, logits.shape, 1)
    neg = -jnp.inf

    def top(vals):
        m = jnp.max(vals, axis=1, keepdims=True)
        idx = jnp.min(jnp.where(vals == m, lane, ROUTE_LANES), axis=1, keepdims=True)
        return m, idx

    is_group = lane < N_GROUPS_MOE
    gmax, grp = top(jnp.where(is_group, logits, neg))
    p_grp = 1.0 / jnp.sum(jnp.where(is_group, jnp.exp(logits - gmax), 0.0), axis=1, keepdims=True)
    first = N_GROUPS_MOE + grp * EXPERTS_PER_GROUP
    local = jnp.where((lane >= first) & (lane < first + EXPERTS_PER_GROUP), logits, neg)
    m1, i1 = top(local)
    m2, i2 = top(jnp.where(lane == i1, neg, local))
    e2 = jnp.exp(m2 - m1)
    w1 = p_grp / (1.0 + e2)
    w2 = p_grp * e2 / (1.0 + e2)
    e_ref[...] = jnp.where(lane == 0, i1 - N_GROUPS_MOE, jnp.where(lane == 1, i2 - N_GROUPS_MOE, 0))
    p_ref[...] = jnp.where(lane == 0, w1, jnp.where(lane == 1, w2, 0.0))


def _route(x1, g_ffn, w_group, b_group, w_router, b_router):
    T, D = x1.shape
    n_logit = w_group.shape[1] + w_router.shape[1]
    assert n_logit <= ROUTE_LANES
    wr = jnp.zeros((D, ROUTE_LANES), F32).at[:, :n_logit].set(jnp.concatenate([w_group, w_router], axis=1))
    br = jnp.zeros((1, ROUTE_LANES), F32).at[0, :n_logit].set(jnp.concatenate([b_group, b_router]))
    tm = _tile(T, 256)
    return pl.pallas_call(
        _route_kernel,
        grid=(T // tm,),
        in_specs=[pl.BlockSpec((tm, D), lambda i: (i, 0)),
                  pl.BlockSpec((1, D), lambda i: (0, 0)),
                  pl.BlockSpec((D, ROUTE_LANES), lambda i: (0, 0)),
                  pl.BlockSpec((1, ROUTE_LANES), lambda i: (0, 0))],
        out_specs=[pl.BlockSpec((tm, D), lambda i: (i, 0)),
                   pl.BlockSpec((tm, ROUTE_LANES), lambda i: (i, 0)),
                   pl.BlockSpec((tm, ROUTE_LANES), lambda i: (i, 0))],
        out_shape=[jax.ShapeDtypeStruct((T, D), F32),
                   jax.ShapeDtypeStruct((T, ROUTE_LANES), jnp.int32),
                   jax.ShapeDtypeStruct((T, ROUTE_LANES), F32)],
        compiler_params=_params(("parallel",)),
        name="route",
    )(x1, g_ffn.reshape(1, D), wr, br)


def _dispatch_tables(e_flat, n_exp, blk, n_blocks):
    A = e_flat.shape[0]
    onehot = (e_flat[:, None] == jnp.arange(n_exp, dtype=jnp.int32)[None, :]).astype(jnp.int32)
    csum = jnp.cumsum(onehot, axis=0)
    counts = csum[-1]
    rank = jnp.sum(csum * onehot, axis=1) - 1
    padded = (counts + blk - 1) // blk * blk
    pad_end = jnp.cumsum(padded)
    pad_start = pad_end - padded
    pos = jnp.sum(onehot * pad_start[None, :], axis=1) + rank
    row_tok = jnp.zeros((n_blocks * blk,), jnp.int32).at[pos].set(
        jnp.arange(A, dtype=jnp.int32) // TOP_K, unique_indices=True)
    bstart = jnp.arange(n_blocks, dtype=jnp.int32) * blk
    be = jnp.minimum(jnp.searchsorted(pad_end, bstart, side='right'), n_exp - 1).astype(jnp.int32)
    nvalid = jnp.clip(counts[be] - (bstart - pad_start[be]), 0, blk).astype(jnp.int32)
    n_used = pad_end[-1] // blk
    be = jnp.where(jnp.arange(n_blocks) < n_used, be, be[n_used - 1])
    return pos.astype(jnp.int32), row_tok, be, nvalid


def _moe_kernel(be_ref, nv_ref, tok_hbm, x_hbm, wg_ref, wu_ref, wd_ref, y_ref,
                idx_smem, xbuf, xb16, idx_sem, row_sem, *, blk, n_blocks):
    i = pl.program_id(0)
    h = pl.program_id(1)

    def idx_copy(b):
        return pltpu.make_async_copy(tok_hbm.at[pl.ds(pl.multiple_of(b * blk, blk), blk)],
                                     idx_smem.at[b % 2], idx_sem)

    def row_copy(tok, r):
        return pltpu.make_async_copy(x_hbm.at[pl.ds(tok, 1), :], xbuf.at[pl.ds(r, 1), :], row_sem)

    def issue_rows(b):
        slot = b % 2

        def body(r, c):
            row_copy(idx_smem[slot, r], r).start()
            return c
        lax.fori_loop(0, blk, body, 0)

    def wait_rows():
        def body(r, c):
            row_copy(0, r).wait()
            return c
        lax.fori_loop(0, blk, body, 0)

    @pl.when((i == 0) & (h == 0))
    def _():
        idx_copy(0).start()
        idx_copy(0).wait()
        issue_rows(0)

    nxt = i + 1
    has_next = (nxt < n_blocks) & (nv_ref[jnp.minimum(nxt, n_blocks - 1)] > 0)
    used = nv_ref[i] > 0

    @pl.when((h == 0) & has_next)
    def _():
        idx_copy(nxt).start()

    @pl.when((h == 0) & used)
    def _():
        wait_rows()
        xb16[...] = xbuf[...].astype(BF16)

    @pl.when((h == 1) & has_next)
    def _():
        idx_copy(nxt).wait()
        issue_rows(nxt)

    @pl.when(used)
    def _():
        xb = xb16[...]
        g = jnp.dot(xb, wg_ref[...], preferred_element_type=F32)
        u = jnp.dot(xb, wu_ref[...], preferred_element_type=F32)
        hid = (g * _sigmoid(g) * u).astype(BF16)
        part = jnp.dot(hid, wd_ref[...], preferred_element_type=F32)

        @pl.when(h == 0)
        def _():
            y_ref[...] = part

        @pl.when(h != 0)
        def _():
            y_ref[...] += part

    @pl.when(jnp.logical_not(used) & (h == 0))
    def _():
        y_ref[...] = jnp.zeros_like(y_ref)


def _moe(xn, row_tok, be, nvalid, w_gate, w_up, w_down, blk, n_blocks):
    T, D = xn.shape
    hid = w_gate.shape[2]
    n_h = 2
    th = hid // n_h
    kern = functools.partial(_moe_kernel, blk=blk, n_blocks=n_blocks)

    def hsel(i, h, nv):
        return jnp.where(nv[i] > 0, h, n_h - 1)

    grid_spec = pltpu.PrefetchScalarGridSpec(
        num_scalar_prefetch=2,
        grid=(n_blocks, n_h),
        in_specs=[pl.BlockSpec(memory_space=pl.ANY),
                  pl.BlockSpec(memory_space=pl.ANY),
                  pl.BlockSpec((None, D, th), lambda i, h, be, nv: (be[i], 0, hsel(i, h, nv))),
                  pl.BlockSpec((None, D, th), lambda i, h, be, nv: (be[i], 0, hsel(i, h, nv))),
                  pl.BlockSpec((None, th, D), lambda i, h, be, nv: (be[i], hsel(i, h, nv), 0))],
        out_specs=pl.BlockSpec((blk, D), lambda i, h, be, nv: (i, 0)),
        scratch_shapes=[pltpu.SMEM((2, blk), jnp.int32),
                        pltpu.VMEM((blk, D), F32),
                        pltpu.VMEM((blk, D), BF16),
                        pltpu.SemaphoreType.DMA(()),
                        pltpu.SemaphoreType.DMA(())],
    )
    return pl.pallas_call(
        kern,
        grid_spec=grid_spec,
        out_shape=jax.ShapeDtypeStruct((n_blocks * blk, D), F32),
        compiler_params=_params(("arbitrary", "arbitrary")),
        name="moe_experts",
    )(be, nvalid, row_tok, xn, w_gate, w_up, w_down)


def _combine_kernel(pos_hbm, y_hbm, x_ref, w_ref, g_ref, o_ref,
                    idx_smem, ybuf, idx_sem, row_sem, *, tm, n_tiles, final_norm):
    i = pl.program_id(0)
    n_idx = TOP_K * tm

    def idx_copy(t):
        return pltpu.make_async_copy(pos_hbm.at[pl.ds(pl.multiple_of(t * n_idx, n_idx), n_idx)],
                                     idx_smem.at[t % 2], idx_sem)

    def row_copy(src, slot, dst):
        return pltpu.make_async_copy(y_hbm.at[pl.ds(src, 1), :], ybuf.at[slot, pl.ds(dst, 1), :],
                                     row_sem.at[slot])

    def issue_rows(t):
        slot = t % 2

        def body(r, c):
            for k in range(TOP_K):
                row_copy(idx_smem[slot, TOP_K * r + k], slot, k * tm + r).start()
            return c
        lax.fori_loop(0, tm, body, 0)

    def wait_rows(slot):
        def body(r, c):
            row_copy(0, slot, r).wait()
            return c
        lax.fori_loop(0, n_idx, body, 0)

    @pl.when(i == 0)
    def _():
        idx_copy(0).start()
        idx_copy(0).wait()
        issue_rows(0)

        if n_tiles > 1:
            idx_copy(1).start()

    @pl.when(i + 1 < n_tiles)
    def _():
        idx_copy(i + 1).wait()
        issue_rows(i + 1)

    @pl.when(i + 2 < n_tiles)
    def _():
        idx_copy(i + 2).start()

    slot = i % 2
    wait_rows(slot)
    w = w_ref[...]
    out = x_ref[...] + w[:, 0:1] * ybuf[slot, 0:tm, :] + w[:, 1:2] * ybuf[slot, tm:2 * tm, :]
    if final_norm:
        out = _rms(out, g_ref[...])
    o_ref[...] = out


def _combine(x1, y, pos, w_assign, g_final, final_norm):
    T, D = x1.shape
    tm = _tile(T, 256)
    n_tiles = T // tm
    kern = functools.partial(_combine_kernel, tm=tm, n_tiles=n_tiles, final_norm=final_norm)
    return pl.pallas_call(
        kern,
        grid=(n_tiles,),
        in_specs=[pl.BlockSpec(memory_space=pl.ANY),
                  pl.BlockSpec(memory_space=pl.ANY),
                  pl.BlockSpec((tm, D), lambda i: (i, 0)),
                  pl.BlockSpec((tm, TOP_K), lambda i: (i, 0)),
                  pl.BlockSpec((1, D), lambda i: (0, 0))],
        out_specs=pl.BlockSpec((tm, D), lambda i: (i, 0)),
        out_shape=jax.ShapeDtypeStruct((T, D), F32),
        scratch_shapes=[pltpu.SMEM((2, TOP_K * tm), jnp.int32),
                        pltpu.VMEM((2, TOP_K * tm, D), F32),
                        pltpu.SemaphoreType.DMA(()),
                        pltpu.SemaphoreType.DMA((2,))],
        compiler_params=_params(("arbitrary",)),
        name="moe_combine",
    )(pos, y, x1, w_assign, g_final.reshape(1, D))


def _layer(x, B, S, p, g_final, final_norm):
    T, D = x.shape
    W = D // 2
    xn = _norm_bf16(x, p["g_mix"])
    proj = _in_proj(xn, p["w_in"].astype(BF16), p["b_gate"], W)
    a = _sgu(proj, p["g_sgu"], p["w_sgu"], p["b_sgu"], W)
    b = _attention(proj, B, S, W)
    merged = _merge(a, b, proj, p["w_proj_a"].astype(BF16), p["w_proj_b"].astype(BF16), W, D)
    x1 = _out_proj(merged, p["w_out"].astype(BF16), x)

    xn2, e_out, p_out = _route(x1, p["g_ffn"], p["w_group"], p["b_group"], p["w_router"], p["b_router"])
    n_exp = p["w_gate"].shape[0]
    blk = 256
    A = T * TOP_K
    n_blocks = A // blk + n_exp
    e_flat = e_out[:, :TOP_K].reshape(A)
    pos, row_tok, be, nvalid = _dispatch_tables(e_flat, n_exp, blk, n_blocks)
    y = _moe(xn2, row_tok, be, nvalid, p["w_gate"].astype(BF16), p["w_up"].astype(BF16),
             p["w_down"].astype(BF16), blk, n_blocks)
    return _combine(x1, y, pos, p_out[:, :TOP_K], g_final, final_norm)


def kernel(x, g_mix, w_in, g_sgu, w_sgu, b_sgu, b_gate, w_proj_a, w_proj_b, w_out,
           g_ffn, w_group, b_group, w_router, b_router, w_gate, w_up, w_down, g_final):
    B, S, D = x.shape
    stacked = dict(g_mix=g_mix, w_in=w_in, g_sgu=g_sgu, w_sgu=w_sgu, b_sgu=b_sgu, b_gate=b_gate,
                   w_proj_a=w_proj_a, w_proj_b=w_proj_b, w_out=w_out, g_ffn=g_ffn,
                   w_group=w_group, b_group=b_group, w_router=w_router, b_router=b_router,
                   w_gate=w_gate, w_up=w_up, w_down=w_down)
    depth = w_in.shape[0]
    xf = x.reshape(B * S, D)
    for l in range(depth):
        p = {name: val[l] for name, val in stacked.items()}
        xf = _layer(xf, B, S, p, g_final, final_norm=(l == depth - 1))
    return xf.reshape(B, S, D)
```

```python
import functools

import jax
import jax.numpy as jnp
from jax import lax
from jax.experimental import pallas as pl
from jax.experimental.pallas import tpu as pltpu

F32 = jnp.float32
BF16 = jnp.bfloat16

EPS = 1e-6
CHUNK = 64
SGU_BLOCK = 128
SGU_GROUPS = 8
HEAD_DIM = 128
N_GROUPS_MOE = 4
EXPERTS_PER_GROUP = 8
TOP_K = 2
ROUTE_LANES = 128

SKIP_BELOW = 120.0

VMEM_LIMIT = 56 * 2**20


def _tile(n, pref):
    t = min(n, pref)
    assert n % t == 0, (n, pref)
    return t


def _params(sem):
    return pltpu.CompilerParams(dimension_semantics=sem, vmem_limit_bytes=VMEM_LIMIT)


def _sigmoid(x):
    return 1.0 / (1.0 + jnp.exp(-x))


def _gelu_tanh(x):
    c = 0.7978845608028654
    return x * (0.5 * (1.0 + jnp.tanh(c * (x + 0.044715 * (x * x * x)))))


def _rms(x, g):
    return x * lax.rsqrt(jnp.mean(x * x, axis=-1, keepdims=True) + EPS) * g


def _norm_kernel(x_ref, g_ref, o_ref):
    o_ref[...] = _rms(x_ref[...], g_ref[...]).astype(o_ref.dtype)


def _norm_bf16(x, g):
    T, D = x.shape
    tm = _tile(T, 512)
    return pl.pallas_call(
        _norm_kernel,
        grid=(T // tm,),
        in_specs=[pl.BlockSpec((tm, D), lambda i: (i, 0)),
                  pl.BlockSpec((1, D), lambda i: (0, 0))],
        out_specs=pl.BlockSpec((tm, D), lambda i: (i, 0)),
        out_shape=jax.ShapeDtypeStruct((T, D), BF16),
        compiler_params=_params(("parallel",)),
        name="norm_bf16",
    )(x, g.reshape(1, D))


def _proj_kernel(x_ref, w_ref, b_ref, o_ref, wb_ref, *, n_gelu, n_q_end, n_plain_end, q_scale):
    j = pl.program_id(0)

    @pl.when(pl.program_id(1) == 0)
    def _():
        wb_ref[...] = w_ref[...].astype(BF16)

    acc = jnp.dot(x_ref[...], wb_ref[...], preferred_element_type=F32)

    @pl.when(j < n_gelu)
    def _():
        o_ref[...] = _gelu_tanh(acc).astype(o_ref.dtype)

    @pl.when((j >= n_gelu) & (j < n_q_end))
    def _():
        o_ref[...] = (acc * q_scale).astype(o_ref.dtype)

    @pl.when((j >= n_q_end) & (j < n_plain_end))
    def _():
        o_ref[...] = acc.astype(o_ref.dtype)

    @pl.when(j >= n_plain_end)
    def _():
        o_ref[...] = _sigmoid(acc + b_ref[...]).astype(o_ref.dtype)


def _in_proj(xn, w_in, b_gate, W):
    T, D = xn.shape
    N = w_in.shape[1]
    tm = _tile(T, 1024)
    tn = _tile(W, 512)
    n_gelu = 2 * W // tn
    n_q_end = 3 * W // tn
    n_plain_end = 5 * W // tn
    kern = functools.partial(_proj_kernel, n_gelu=n_gelu, n_q_end=n_q_end,
                             n_plain_end=n_plain_end, q_scale=HEAD_DIM ** -0.5)
    return pl.pallas_call(
        kern,
        grid=(N // tn, T // tm),
        in_specs=[pl.BlockSpec((tm, D), lambda j, i: (i, 0)),
                  pl.BlockSpec((D, tn), lambda j, i: (0, j)),
                  pl.BlockSpec((1, tn), lambda j, i: (0, jnp.maximum(j - n_plain_end, 0)))],
        out_specs=pl.BlockSpec((tm, tn), lambda j, i: (i, j)),
        out_shape=jax.ShapeDtypeStruct((T, N), BF16),
        scratch_shapes=[pltpu.VMEM((D, tn), BF16)],
        compiler_params=_params(("parallel", "arbitrary")),
        name="in_proj",
    )(xn, w_in, b_gate.reshape(1, -1))


def _sgu_kernel(u_ref, v_ref, g_ref, w_ref, bt_ref, o_ref, *, n_sub, gd):
    v = v_ref[...].astype(F32)
    vn = _rms(v, g_ref[...]).astype(BF16)
    t_chunk = lax.broadcasted_iota(jnp.int32, (SGU_BLOCK, SGU_BLOCK), 0) // CHUNK
    s_chunk = lax.broadcasted_iota(jnp.int32, (SGU_BLOCK, SGU_BLOCK), 1) // CHUNK
    causal = s_chunk <= t_chunk
    for g in range(SGU_GROUPS):
        w = jnp.where(causal, w_ref[g], 0.0).astype(BF16)
        bias = bt_ref[:, g:g + 1]
        cols = slice(g * gd, (g + 1) * gd)
        for n in range(n_sub):
            rows = slice(n * SGU_BLOCK, (n + 1) * SGU_BLOCK)
            mixed = jnp.dot(w, vn[rows, cols], preferred_element_type=F32) + bias
            o_ref[rows, cols] = (u_ref[rows, cols].astype(F32) * mixed).astype(o_ref.dtype)


def _sgu(proj, g_sgu, w_sgu, b_sgu, W):
    T = proj.shape[0]
    tm = _tile(T, 2 * SGU_BLOCK)
    gd = W // SGU_GROUPS
    kern = functools.partial(_sgu_kernel, n_sub=tm // SGU_BLOCK, gd=gd)
    return pl.pallas_call(
        kern,
        grid=(T // tm,),
        in_specs=[pl.BlockSpec((tm, W), lambda i: (i, 0)),
                  pl.BlockSpec((tm, W), lambda i: (i, 1)),
                  pl.BlockSpec((1, W), lambda i: (0, 0)),
                  pl.BlockSpec((SGU_GROUPS, SGU_BLOCK, SGU_BLOCK), lambda i: (0, 0, 0)),
                  pl.BlockSpec((SGU_BLOCK, SGU_GROUPS), lambda i: (0, 0))],
        out_specs=pl.BlockSpec((tm, W), lambda i: (i, 0)),
        out_shape=jax.ShapeDtypeStruct((T, W), BF16),
        compiler_params=_params(("parallel",)),
        name="sgu",
    )(proj, proj, g_sgu.reshape(1, W), w_sgu, b_sgu.T)


def _attn_kernel(q_ref, k_ref, v_ref, o_ref, acc_ref, carry_ref, *, tq, heads):
    qi = pl.program_id(2)
    jj = lax.broadcasted_iota(jnp.int32, (tq, tq), 0)
    ss = lax.broadcasted_iota(jnp.int32, (tq, tq), 1)
    tri = (jj >= ss).astype(BF16)
    before = ss < jj

    def step(kb, diagonal):
        rows = pl.ds(pl.multiple_of(kb * tq, tq), tq)
        worst = None
        for h in range(heads):
            cols = slice(h * HEAD_DIM, (h + 1) * HEAD_DIM)
            z = lax.dot_general(q_ref[:, cols], k_ref[rows, cols], (((1,), (1,)), ((), ())),
                                preferred_element_type=F32)
            lg = -(jnp.maximum(z, 0.0) + jnp.log(1.0 + jnp.exp(-jnp.abs(z))))
            if diagonal:
                lg = jnp.where(before, lg, 0.0)
            hi = lg.astype(BF16)
            lo = (lg - hi.astype(F32)).astype(BF16)
            incl = (jnp.dot(hi, tri, preferred_element_type=F32)
                    + jnp.dot(lo, tri, preferred_element_type=F32))
            if diagonal:
                a = jnp.where(before, jnp.exp(z + incl), 0.0)
                carry = incl[:, 0:1]
            else:
                a = jnp.exp(z + incl + carry_ref[h])
                carry = carry_ref[h] + incl[:, 0:1]
            pv = jnp.dot(a.astype(BF16), v_ref[rows, cols], preferred_element_type=F32)
            if diagonal:
                acc_ref[:, cols] = pv
            else:
                acc_ref[:, cols] += pv
            carry_ref[h] = carry
            worst = carry if worst is None else jnp.maximum(worst, carry)
        return jnp.max(worst)

    def cond(state):
        kb, worst = state
        return (kb >= 0) & (worst > -SKIP_BELOW)

    def body(state):
        kb, _ = state
        return kb - 1, step(kb, False)

    lax.while_loop(cond, body, (qi - 1, step(qi, True)))
    o_ref[...] = acc_ref[...].astype(o_ref.dtype)


def _attention(proj, B, S, W):
    T = proj.shape[0]
    n_heads = W // HEAD_DIM
    heads = min(4, n_heads)
    assert n_heads % heads == 0
    hw = heads * HEAD_DIM
    tq = _tile(S, 256)
    nq = S // tq
    q0, k0, v0 = 2 * W // hw, 3 * W // hw, 4 * W // hw
    kern = functools.partial(_attn_kernel, tq=tq, heads=heads)
    return pl.pallas_call(
        kern,
        grid=(B, n_heads // heads, nq),
        in_specs=[pl.BlockSpec((tq, hw), lambda b, g, i: (b * nq + i, q0 + g)),
                  pl.BlockSpec((S, hw), lambda b, g, i: (b, k0 + g)),
                  pl.BlockSpec((S, hw), lambda b, g, i: (b, v0 + g))],
        out_specs=pl.BlockSpec((tq, hw), lambda b, g, i: (b * nq + i, g)),
        out_shape=jax.ShapeDtypeStruct((T, W), BF16),
        scratch_shapes=[pltpu.VMEM((tq, hw), F32),
                        pltpu.VMEM((heads, tq, 1), F32)],
        compiler_params=_params(("parallel", "parallel", "arbitrary")),
        name="stick_breaking",
    )(proj, proj, proj)


def _merge_kernel(a_ref, b_ref, pa_ref, pb_ref, ga_ref, gb_ref, o_ref, pab_ref, pbb_ref):
    @pl.when(pl.program_id(1) == 0)
    def _():
        pab_ref[...] = pa_ref[...].astype(BF16)
        pbb_ref[...] = pb_ref[...].astype(BF16)

    ya = jnp.dot(a_ref[...], pab_ref[...], preferred_element_type=F32)
    yb = jnp.dot(b_ref[...], pbb_ref[...], preferred_element_type=F32)
    o_ref[...] = (ga_ref[...].astype(F32) * ya + gb_ref[...].astype(F32) * yb).astype(o_ref.dtype)


def _merge(a, b, proj, w_pa, w_pb, W, D):
    T = a.shape[0]
    tm = _tile(T, 1024)
    tn = _tile(W, 512)
    ga0 = 5 * W // tn
    gb0 = (5 * W + D) // tn
    return pl.pallas_call(
        _merge_kernel,
        grid=(D // tn, T // tm),
        in_specs=[pl.BlockSpec((tm, W), lambda j, i: (i, 0)),
                  pl.BlockSpec((tm, W), lambda j, i: (i, 0)),
                  pl.BlockSpec((W, tn), lambda j, i: (0, j)),
                  pl.BlockSpec((W, tn), lambda j, i: (0, j)),
                  pl.BlockSpec((tm, tn), lambda j, i: (i, ga0 + j)),
                  pl.BlockSpec((tm, tn), lambda j, i: (i, gb0 + j))],
        out_specs=pl.BlockSpec((tm, tn), lambda j, i: (i, j)),
        out_shape=jax.ShapeDtypeStruct((T, D), BF16),
        scratch_shapes=[pltpu.VMEM((W, tn), BF16), pltpu.VMEM((W, tn), BF16)],
        compiler_params=_params(("parallel", "arbitrary")),
        name="merge",
    )(a, b, w_pa, w_pb, proj, proj)


def _out_kernel(m_ref, w_ref, x_ref, o_ref, wb_ref):
    @pl.when(pl.program_id(1) == 0)
    def _():
        wb_ref[...] = w_ref[...].astype(BF16)

    o_ref[...] = x_ref[...] + jnp.dot(m_ref[...], wb_ref[...], preferred_element_type=F32)


def _out_proj(merged, w_out, x):
    T, D = x.shape
    tm = _tile(T, 1024)
    tn = _tile(D, 512)
    return pl.pallas_call(
        _out_kernel,
        grid=(D // tn, T // tm),
        in_specs=[pl.BlockSpec((tm, D), lambda j, i: (i, 0)),
                  pl.BlockSpec((D, tn), lambda j, i: (0, j)),
                  pl.BlockSpec((tm, tn), lambda j, i: (i, j))],
        out_specs=pl.BlockSpec((tm, tn), lambda j, i: (i, j)),
        out_shape=jax.ShapeDtypeStruct((T, D), F32),
        scratch_shapes=[pltpu.VMEM((D, tn), BF16)],
        compiler_params=_params(("parallel", "arbitrary")),
        name="out_proj",
    )(merged, w_out, x)


def _route_kernel(x_ref, g_ref, wr_ref, br_ref, xn_ref, e_ref, p_ref, c_ref):
    xn = _rms(x_ref[...], g_ref[...])
    xn_ref[...] = xn
    logits = jnp.dot(xn, wr_ref[...], precision=lax.Precision.HIGHEST,
                     preferred_element_type=F32) + br_ref[...]
    tm = logits.shape[0]
    lane = lax.broadcasted_iota(jnp.int32, logits.shape, 1)
    neg = -jnp.inf

    def top(vals):
        m = jnp.max(vals, axis=1, keepdims=True)
        idx = jnp.min(jnp.where(vals == m, lane, ROUTE_LANES), axis=1, keepdims=True)
        return m, idx

    is_group = lane < N_GROUPS_MOE
    gmax, grp = top(jnp.where(is_group, logits, neg))
    p_grp = 1.0 / jnp.sum(jnp.where(is_group, jnp.exp(logits - gmax), 0.0), axis=1, keepdims=True)
    first = N_GROUPS_MOE + grp * EXPERTS_PER_GROUP
    local = jnp.where((lane >= first) & (lane < first + EXPERTS_PER_GROUP), logits, neg)
    m1, i1 = top(local)
    m2, i2 = top(jnp.where(lane == i1, neg, local))
    e2 = jnp.exp(m2 - m1)
    w1 = p_grp / (1.0 + e2)
    w2 = p_grp * e2 / (1.0 + e2)
    ex1 = i1 - N_GROUPS_MOE
    ex2 = i2 - N_GROUPS_MOE

    hot1 = lane == ex1
    hot2 = lane == ex2
    chosen = (hot1 | hot2).astype(BF16)
    row = lax.broadcasted_iota(jnp.int32, (tm, tm), 0)
    col = lax.broadcasted_iota(jnp.int32, (tm, tm), 1)
    earlier = jnp.dot((col < row).astype(BF16), chosen, preferred_element_type=F32)
    r1 = jnp.sum(jnp.where(hot1, earlier, 0.0), axis=1, keepdims=True).astype(jnp.int32)
    r2 = jnp.sum(jnp.where(hot2, earlier, 0.0), axis=1, keepdims=True).astype(jnp.int32)
    counts = jnp.sum(chosen.astype(F32), axis=0, keepdims=True).astype(jnp.int32)

    e_ref[...] = jnp.where(lane == 0, ex1, jnp.where(lane == 1, ex2,
                           jnp.where(lane == 2, r1, jnp.where(lane == 3, r2, 0))))
    p_ref[...] = jnp.where(lane == 0, w1, jnp.where(lane == 1, w2, 0.0))
    c_ref[...] = jnp.broadcast_to(counts[None], c_ref.shape)


def _route(x1, g_ffn, w_group, b_group, w_router, b_router):
    T, D = x1.shape
    n_logit = w_group.shape[1] + w_router.shape[1]
    assert n_logit <= ROUTE_LANES
    wr = jnp.zeros((D, ROUTE_LANES), F32).at[:, :n_logit].set(jnp.concatenate([w_group, w_router], axis=1))
    br = jnp.zeros((1, ROUTE_LANES), F32).at[0, :n_logit].set(jnp.concatenate([b_group, b_router]))
    tm = _tile(T, 256)
    n_tiles = T // tm
    return tm, pl.pallas_call(
        _route_kernel,
        grid=(n_tiles,),
        in_specs=[pl.BlockSpec((tm, D), lambda i: (i, 0)),
                  pl.BlockSpec((1, D), lambda i: (0, 0)),
                  pl.BlockSpec((D, ROUTE_LANES), lambda i: (0, 0)),
                  pl.BlockSpec((1, ROUTE_LANES), lambda i: (0, 0))],
        out_specs=[pl.BlockSpec((tm, D), lambda i: (i, 0)),
                   pl.BlockSpec((tm, ROUTE_LANES), lambda i: (i, 0)),
                   pl.BlockSpec((tm, ROUTE_LANES), lambda i: (i, 0)),
                   pl.BlockSpec((1, 8, ROUTE_LANES), lambda i: (i, 0, 0))],
        out_shape=[jax.ShapeDtypeStruct((T, D), F32),
                   jax.ShapeDtypeStruct((T, ROUTE_LANES), jnp.int32),
                   jax.ShapeDtypeStruct((T, ROUTE_LANES), F32),
                   jax.ShapeDtypeStruct((n_tiles, 8, ROUTE_LANES), jnp.int32)],
        compiler_params=_params(("parallel",)),
        name="route",
    )(x1, g_ffn.reshape(1, D), wr, br)


def _dispatch_tables(experts, ranks, tile_counts, route_tm, n_exp, blk, n_blocks, n_split):
    T = experts.shape[0]
    i32 = jnp.int32
    tile_base = jnp.cumsum(tile_counts, axis=0) - tile_counts
    counts = jnp.sum(tile_counts, axis=0)
    padded = (counts + blk - 1) // blk * blk
    pad_end = jnp.cumsum(padded)
    pad_start = pad_end - padded
    base = jnp.repeat(tile_base + pad_start[None, :], route_tm, axis=0)
    hot = experts[:, :, None] == jnp.arange(n_exp, dtype=i32)[None, None, :]
    pos = (jnp.sum(jnp.where(hot, base[:, None, :], 0), axis=2) + ranks).astype(i32)
    token = jnp.broadcast_to(jnp.arange(T, dtype=i32)[:, None], pos.shape)
    row_tok = jnp.zeros((n_blocks * blk,), i32).at[pos.reshape(-1)].set(
        token.reshape(-1), unique_indices=True)

    n_used = pad_end[-1] // blk
    bstart = jnp.arange(n_blocks, dtype=i32) * blk
    be = jnp.minimum(jnp.sum((pad_end[None, :] <= bstart[:, None]).astype(i32), axis=1), n_exp - 1)
    be = jnp.where(jnp.arange(n_blocks) < n_used, be, be[n_used - 1])

    k = jnp.arange(n_blocks * n_split, dtype=i32)
    e_k = be[k // n_split]
    first_blk = pad_start[e_k] // blk
    n_blk = jnp.maximum(padded[e_k] // blk, 1)
    p = k - n_split * first_blk
    used = k < n_split * n_used
    idle = k - n_split * n_used
    item_blk = jnp.where(used, first_blk + p % n_blk, n_used + idle // n_split)
    item_split = jnp.where(used, p // n_blk, idle % n_split)
    item_wsplit = jnp.where(used, item_split, n_split - 1)
    item_first = used & (p % n_blk == 0)
    flags = used.astype(i32) + 2 * item_first.astype(i32)
    items = (item_blk.astype(i32), item_split.astype(i32), item_wsplit.astype(i32), e_k.astype(i32), flags)

    blk_id = jnp.arange(n_blocks, dtype=i32)
    blk_used = blk_id < n_used
    blk_first = blk_used & (blk_id == pad_start[be] // blk)
    block_flags = blk_used.astype(i32) + 2 * blk_first.astype(i32)
    return pos, row_tok, items, be.astype(i32), block_flags


def _dispatch_kernel(tok_hbm, x_hbm, o_ref, idx_smem, xbuf, idx_sem, row_sem, *, blk, n_blocks):
    i = pl.program_id(0)

    def idx_copy(b):
        return pltpu.make_async_copy(tok_hbm.at[pl.ds(pl.multiple_of(b * blk, blk), blk)],
                                     idx_smem.at[b % 2], idx_sem)

    def row_copy(tok, slot, r):
        return pltpu.make_async_copy(x_hbm.at[pl.ds(tok, 1), :], xbuf.at[slot, pl.ds(r, 1), :],
                                     row_sem.at[slot])

    def issue_rows(b):
        slot = b % 2

        def body(r, c):
            row_copy(idx_smem[slot, r], slot, r).start()
            return c
        lax.fori_loop(0, blk, body, 0, unroll=16)

    def wait_rows(slot):
        pltpu.make_async_copy(x_hbm.at[pl.ds(0, blk), :], xbuf.at[slot], row_sem.at[slot]).wait()

    @pl.when(i == 0)
    def _():
        idx_copy(0).start()
        idx_copy(0).wait()
        issue_rows(0)
        if n_blocks > 1:
            idx_copy(1).start()

    @pl.when(i + 1 < n_blocks)
    def _():
        idx_copy(i + 1).wait()
        issue_rows(i + 1)

    @pl.when(i + 2 < n_blocks)
    def _():
        idx_copy(i + 2).start()

    slot = i % 2
    wait_rows(slot)
    o_ref[...] = xbuf[slot].astype(o_ref.dtype)


def _dispatch(xn, row_tok, blk, n_blocks):
    T, D = xn.shape
    kern = functools.partial(_dispatch_kernel, blk=blk, n_blocks=n_blocks)
    return pl.pallas_call(
        kern,
        grid=(n_blocks,),
        in_specs=[pl.BlockSpec(memory_space=pl.ANY),
                  pl.BlockSpec(memory_space=pl.ANY)],
        out_specs=pl.BlockSpec((blk, D), lambda i: (i, 0)),
        out_shape=jax.ShapeDtypeStruct((n_blocks * blk, D), BF16),
        scratch_shapes=[pltpu.SMEM((2, blk), jnp.int32),
                        pltpu.VMEM((2, blk, D), F32),
                        pltpu.SemaphoreType.DMA(()),
                        pltpu.SemaphoreType.DMA((2,))],
        compiler_params=_params(("arbitrary",)),
        name="moe_dispatch",
    )(row_tok, xn)


def _moe_up_kernel(blk_ref, split_ref, wsplit_ref, e_ref, flag_ref,
                   x_ref, wg_ref, wu_ref, o_ref, wgb_ref, wub_ref):
    flag = flag_ref[pl.program_id(0)]

    @pl.when((flag & 2) != 0)
    def _():
        wgb_ref[...] = wg_ref[...].astype(BF16)
        wub_ref[...] = wu_ref[...].astype(BF16)

    @pl.when((flag & 1) != 0)
    def _():
        x = x_ref[...]
        g = jnp.dot(x, wgb_ref[...], preferred_element_type=F32)
        u = jnp.dot(x, wub_ref[...], preferred_element_type=F32)
        o_ref[...] = (g * _sigmoid(g) * u).astype(o_ref.dtype)

    @pl.when((flag & 1) == 0)
    def _():
        o_ref[...] = jnp.zeros_like(o_ref)


def _moe_up(xs, items, w_gate, w_up, blk, n_split):
    n_rows, D = xs.shape
    hid = w_gate.shape[2]
    th = hid // n_split
    n_items = items[0].shape[0]
    grid_spec = pltpu.PrefetchScalarGridSpec(
        num_scalar_prefetch=5,
        grid=(n_items,),
        in_specs=[pl.BlockSpec((blk, D), lambda k, b, s, ws, e, f: (b[k], 0)),
                  pl.BlockSpec((None, D, th), lambda k, b, s, ws, e, f: (e[k], 0, ws[k])),
                  pl.BlockSpec((None, D, th), lambda k, b, s, ws, e, f: (e[k], 0, ws[k]))],
        out_specs=pl.BlockSpec((blk, th), lambda k, b, s, ws, e, f: (b[k], s[k])),
        scratch_shapes=[pltpu.VMEM((D, th), BF16), pltpu.VMEM((D, th), BF16)],
    )
    return pl.pallas_call(
        _moe_up_kernel,
        grid_spec=grid_spec,
        out_shape=jax.ShapeDtypeStruct((n_rows, hid), BF16),
        compiler_params=_params(("arbitrary",)),
        name="moe_up",
    )(*items, xs, w_gate, w_up)


def _moe_down_kernel(e_ref, flag_ref, h_ref, wd_ref, o_ref, wdb_ref):
    flag = flag_ref[pl.program_id(0)]

    @pl.when((flag & 2) != 0)
    def _():
        wdb_ref[...] = wd_ref[...].astype(BF16)

    @pl.when((flag & 1) != 0)
    def _():
        o_ref[...] = jnp.dot(h_ref[...], wdb_ref[...], preferred_element_type=F32)

    @pl.when((flag & 1) == 0)
    def _():
        o_ref[...] = jnp.zeros_like(o_ref)


def _moe_down(hid, block_expert, block_flags, w_down, blk):
    n_rows, H = hid.shape
    D = w_down.shape[2]
    grid_spec = pltpu.PrefetchScalarGridSpec(
        num_scalar_prefetch=2,
        grid=(n_rows // blk,),
        in_specs=[pl.BlockSpec((blk, H), lambda i, e, f: (i, 0)),
                  pl.BlockSpec((None, H, D), lambda i, e, f: (e[i], 0, 0))],
        out_specs=pl.BlockSpec((blk, D), lambda i, e, f: (i, 0)),
        scratch_shapes=[pltpu.VMEM((H, D), BF16)],
    )
    return pl.pallas_call(
        _moe_down_kernel,
        grid_spec=grid_spec,
        out_shape=jax.ShapeDtypeStruct((n_rows, D), F32),
        compiler_params=_params(("arbitrary",)),
        name="moe_down",
    )(block_expert, block_flags, hid, w_down)


def _combine_kernel(pos_hbm, y_hbm, x_ref, w_ref, g_ref, o_ref,
                    idx_smem, ybuf, idx_sem, row_sem, *, tm, n_tiles, final_norm):
    i = pl.program_id(0)
    n_idx = TOP_K * tm

    def idx_copy(t):
        return pltpu.make_async_copy(pos_hbm.at[pl.ds(pl.multiple_of(t * n_idx, n_idx), n_idx)],
                                     idx_smem.at[t % 2], idx_sem)

    def row_copy(src, slot, dst):
        return pltpu.make_async_copy(y_hbm.at[pl.ds(src, 1), :], ybuf.at[slot, pl.ds(dst, 1), :],
                                     row_sem.at[slot])

    def issue_rows(t):
        slot = t % 2

        def body(r, c):
            for k in range(TOP_K):
                row_copy(idx_smem[slot, TOP_K * r + k], slot, k * tm + r).start()
            return c
        lax.fori_loop(0, tm, body, 0, unroll=8)

    def wait_rows(slot):
        pltpu.make_async_copy(y_hbm.at[pl.ds(0, n_idx), :], ybuf.at[slot], row_sem.at[slot]).wait()

    @pl.when(i == 0)
    def _():
        idx_copy(0).start()
        idx_copy(0).wait()
        issue_rows(0)
        if n_tiles > 1:
            idx_copy(1).start()

    @pl.when(i + 1 < n_tiles)
    def _():
        idx_copy(i + 1).wait()
        issue_rows(i + 1)

    @pl.when(i + 2 < n_tiles)
    def _():
        idx_copy(i + 2).start()

    slot = i % 2
    wait_rows(slot)
    w = w_ref[...]
    out = x_ref[...] + w[:, 0:1] * ybuf[slot, 0:tm, :] + w[:, 1:2] * ybuf[slot, tm:2 * tm, :]
    if final_norm:
        out = _rms(out, g_ref[...])
    o_ref[...] = out


def _combine(x1, y, pos, w_assign, g_final, final_norm):
    T, D = x1.shape
    tm = _tile(T, 256)
    n_tiles = T // tm
    kern = functools.partial(_combine_kernel, tm=tm, n_tiles=n_tiles, final_norm=final_norm)
    return pl.pallas_call(
        kern,
        grid=(n_tiles,),
        in_specs=[pl.BlockSpec(memory_space=pl.ANY),
                  pl.BlockSpec(memory_space=pl.ANY),
                  pl.BlockSpec((tm, D), lambda i: (i, 0)),
                  pl.BlockSpec((tm, TOP_K), lambda i: (i, 0)),
                  pl.BlockSpec((1, D), lambda i: (0, 0))],
        out_specs=pl.BlockSpec((tm, D), lambda i: (i, 0)),
        out_shape=jax.ShapeDtypeStruct((T, D), F32),
        scratch_shapes=[pltpu.SMEM((2, TOP_K * tm), jnp.int32),
                        pltpu.VMEM((2, TOP_K * tm, D), F32),
                        pltpu.SemaphoreType.DMA(()),
                        pltpu.SemaphoreType.DMA((2,))],
        compiler_params=_params(("arbitrary",)),
        name="moe_combine",
    )(pos.reshape(-1), y, x1, w_assign, g_final.reshape(1, D))


def _layer(x, B, S, p, g_final, final_norm):
    T, D = x.shape
    W = D // 2
    xn = _norm_bf16(x, p["g_mix"])
    proj = _in_proj(xn, p["w_in"], p["b_gate"], W)
    a = _sgu(proj, p["g_sgu"], p["w_sgu"], p["b_sgu"], W)
    b = _attention(proj, B, S, W)
    merged = _merge(a, b, proj, p["w_proj_a"], p["w_proj_b"], W, D)
    x1 = _out_proj(merged, p["w_out"], x)

    route_tm, (xn2, e_out, p_out, tile_counts) = _route(
        x1, p["g_ffn"], p["w_group"], p["b_group"], p["w_router"], p["b_router"])
    n_exp = p["w_gate"].shape[0]
    blk = 256
    n_split = 2
    n_blocks = T * TOP_K // blk + n_exp
    pos, row_tok, items, block_expert, block_flags = _dispatch_tables(
        e_out[:, :TOP_K], e_out[:, TOP_K:2 * TOP_K], tile_counts[:, 0, :n_exp],
        route_tm, n_exp, blk, n_blocks, n_split)
    xs = _dispatch(xn2, row_tok, blk, n_blocks)
    hid = _moe_up(xs, items, p["w_gate"], p["w_up"], blk, n_split)
    y = _moe_down(hid, block_expert, block_flags, p["w_down"], blk)
    return _combine(x1, y, pos, p_out[:, :TOP_K], g_final, final_norm)


def kernel(x, g_mix, w_in, g_sgu, w_sgu, b_sgu, b_gate, w_proj_a, w_proj_b, w_out,
           g_ffn, w_group, b_group, w_router, b_router, w_gate, w_up, w_down, g_final):
    B, S, D = x.shape
    stacked = dict(g_mix=g_mix, w_in=w_in, g_sgu=g_sgu, w_sgu=w_sgu, b_sgu=b_sgu, b_gate=b_gate,
                   w_proj_a=w_proj_a, w_proj_b=w_proj_b, w_out=w_out, g_ffn=g_ffn,
                   w_group=w_group, b_group=b_group, w_router=w_router, b_router=b_router,
                   w_gate=w_gate, w_up=w_up, w_down=w_down)
    depth = w_in.shape[0]
    xf = x.reshape(B * S, D)
    for l in range(depth):
        p = {name: val[l] for name, val in stacked.items()}
        xf = _layer(xf, B, S, p, g_final, final_norm=(l == depth - 1))
    return xf.reshape(B, S, D)
```

```python
import functools

import jax
import jax.numpy as jnp
from jax import lax
from jax.experimental import pallas as pl
from jax.experimental.pallas import tpu as pltpu

F32 = jnp.float32
BF16 = jnp.bfloat16

EPS = 1e-6
CHUNK = 64
SGU_BLOCK = 128
SGU_GROUPS = 8
HEAD_DIM = 128
N_GROUPS_MOE = 4
EXPERTS_PER_GROUP = 8
TOP_K = 2
ROUTE_LANES = 128

SKIP_BELOW = 120.0

VMEM_LIMIT = 56 * 2**20


def _tile(n, pref):
    t = min(n, pref)
    assert n % t == 0, (n, pref)
    return t


def _params(sem):
    return pltpu.CompilerParams(dimension_semantics=sem, vmem_limit_bytes=VMEM_LIMIT)


def _sigmoid(x):
    return 1.0 / (1.0 + jnp.exp(-x))


def _rms(x, g):
    return x * lax.rsqrt(jnp.mean(x * x, axis=-1, keepdims=True) + EPS) * g


def _norm_kernel(x_ref, g_ref, o_ref):
    o_ref[...] = _rms(x_ref[...], g_ref[...]).astype(o_ref.dtype)


def _norm_bf16(x, g):
    T, D = x.shape
    tm = _tile(T, 512)
    return pl.pallas_call(
        _norm_kernel,
        grid=(T // tm,),
        in_specs=[pl.BlockSpec((tm, D), lambda i: (i, 0)),
                  pl.BlockSpec((1, D), lambda i: (0, 0))],
        out_specs=pl.BlockSpec((tm, D), lambda i: (i, 0)),
        out_shape=jax.ShapeDtypeStruct((T, D), BF16),
        compiler_params=_params(("parallel",)),
        name="norm_bf16",
    )(x, g.reshape(1, D))


def _proj_kernel(x_ref, w_ref, c_ref, o_ref, wb_ref, *, n_chunks):
    @pl.when(pl.program_id(1) == 0)
    def _():
        wb_ref[...] = w_ref[...].astype(BF16)

    cw = o_ref.shape[1] // n_chunks
    for c in range(n_chunks):
        cols = slice(c * cw, (c + 1) * cw)
        y = jnp.dot(x_ref[...], wb_ref[:, cols], preferred_element_type=F32)
        a0, a1 = c_ref[0:1, cols], c_ref[1:2, cols]
        b0, b1, b3 = c_ref[2:3, cols], c_ref[3:4, cols], c_ref[4:5, cols]
        o_ref[:, cols] = ((a0 + a1 * y) * _sigmoid(b0 + y * (b1 + b3 * (y * y)))).astype(o_ref.dtype)


def _proj_epilogue_coefficients(b_gate, W):
    c2 = 2.0 * 0.7978845608028654
    n_gate = b_gate.size
    seg = lambda v, n: jnp.full((n,), v, F32)
    a0 = jnp.concatenate([seg(0.0, 5 * W), seg(1.0, n_gate)])
    a1 = jnp.concatenate([seg(1.0, 2 * W), seg(HEAD_DIM ** -0.5, W), seg(1.0, 2 * W), seg(0.0, n_gate)])
    b0 = jnp.concatenate([seg(0.0, 2 * W), seg(1e30, 3 * W), b_gate.reshape(-1).astype(F32)])
    b1 = jnp.concatenate([seg(c2, 2 * W), seg(0.0, 3 * W), seg(1.0, n_gate)])
    b3 = jnp.concatenate([seg(c2 * 0.044715, 2 * W), seg(0.0, 3 * W + n_gate)])
    zero = jnp.zeros_like(a0)
    return jnp.stack([a0, a1, b0, b1, b3, zero, zero, zero])


def _in_proj(xn, w_in, b_gate, W):
    T, D = xn.shape
    N = w_in.shape[1]
    tm = _tile(T, 1024)
    tn = _tile(W, 512)
    coef = _proj_epilogue_coefficients(b_gate, W)
    assert coef.shape == (8, N)
    kern = functools.partial(_proj_kernel, n_chunks=max(tn // 256, 1))
    return pl.pallas_call(
        kern,
        grid=(N // tn, T // tm),
        in_specs=[pl.BlockSpec((tm, D), lambda j, i: (i, 0)),
                  pl.BlockSpec((D, tn), lambda j, i: (0, j)),
                  pl.BlockSpec((8, tn), lambda j, i: (0, j))],
        out_specs=pl.BlockSpec((tm, tn), lambda j, i: (i, j)),
        out_shape=jax.ShapeDtypeStruct((T, N), BF16),
        scratch_shapes=[pltpu.VMEM((D, tn), BF16)],
        compiler_params=_params(("parallel", "arbitrary")),
        name="in_proj",
    )(xn, w_in, coef)


def _sgu_kernel(u_ref, v_ref, g_ref, w_ref, bt_ref, o_ref, *, n_sub, gd):
    v = v_ref[...].astype(F32)
    vn = _rms(v, g_ref[...]).astype(BF16)
    t_chunk = lax.broadcasted_iota(jnp.int32, (SGU_BLOCK, SGU_BLOCK), 0) // CHUNK
    s_chunk = lax.broadcasted_iota(jnp.int32, (SGU_BLOCK, SGU_BLOCK), 1) // CHUNK
    causal = s_chunk <= t_chunk
    for g in range(SGU_GROUPS):
        w = jnp.where(causal, w_ref[g], 0.0).astype(BF16)
        bias = bt_ref[:, g:g + 1]
        cols = slice(g * gd, (g + 1) * gd)
        for n in range(n_sub):
            rows = slice(n * SGU_BLOCK, (n + 1) * SGU_BLOCK)
            mixed = jnp.dot(w, vn[rows, cols], preferred_element_type=F32) + bias
            o_ref[rows, cols] = (u_ref[rows, cols].astype(F32) * mixed).astype(o_ref.dtype)


def _sgu(proj, g_sgu, w_sgu, b_sgu, W):
    T = proj.shape[0]
    tm = _tile(T, 2 * SGU_BLOCK)
    gd = W // SGU_GROUPS
    kern = functools.partial(_sgu_kernel, n_sub=tm // SGU_BLOCK, gd=gd)
    return pl.pallas_call(
        kern,
        grid=(T // tm,),
        in_specs=[pl.BlockSpec((tm, W), lambda i: (i, 0)),
                  pl.BlockSpec((tm, W), lambda i: (i, 1)),
                  pl.BlockSpec((1, W), lambda i: (0, 0)),
                  pl.BlockSpec((SGU_GROUPS, SGU_BLOCK, SGU_BLOCK), lambda i: (0, 0, 0)),
                  pl.BlockSpec((SGU_BLOCK, SGU_GROUPS), lambda i: (0, 0))],
        out_specs=pl.BlockSpec((tm, W), lambda i: (i, 0)),
        out_shape=jax.ShapeDtypeStruct((T, W), BF16),
        compiler_params=_params(("parallel",)),
        name="sgu",
    )(proj, proj, g_sgu.reshape(1, W), w_sgu, b_sgu.T)


def _attn_kernel(q_ref, k_ref, v_ref, o_ref, acc_ref, carry_ref, *, tq, heads):
    qi = pl.program_id(2)
    jj = lax.broadcasted_iota(jnp.int32, (tq, tq), 0)
    ss = lax.broadcasted_iota(jnp.int32, (tq, tq), 1)
    tri = (jj >= ss).astype(BF16)
    before = ss < jj

    def step(kb, diagonal):
        rows = pl.ds(pl.multiple_of(kb * tq, tq), tq)
        worst = None
        for h in range(heads):
            cols = slice(h * HEAD_DIM, (h + 1) * HEAD_DIM)
            z = lax.dot_general(q_ref[:, cols], k_ref[rows, cols], (((1,), (1,)), ((), ())),
                                preferred_element_type=F32)
            lg = -(jnp.maximum(z, 0.0) + jnp.log(1.0 + jnp.exp(-jnp.abs(z))))
            if diagonal:
                lg = jnp.where(before, lg, 0.0)
            hi = lg.astype(BF16)
            lo = (lg - hi.astype(F32)).astype(BF16)
            incl = (jnp.dot(hi, tri, preferred_element_type=F32)
                    + jnp.dot(lo, tri, preferred_element_type=F32))
            if diagonal:
                a = jnp.where(before, jnp.exp(z + incl), 0.0)
                carry = incl[:, 0:1]
            else:
                a = jnp.exp(z + incl + carry_ref[h])
                carry = carry_ref[h] + incl[:, 0:1]
            pv = jnp.dot(a.astype(BF16), v_ref[rows, cols], preferred_element_type=F32)
            if diagonal:
                acc_ref[:, cols] = pv
            else:
                acc_ref[:, cols] += pv
            carry_ref[h] = carry
            worst = carry if worst is None else jnp.maximum(worst, carry)
        return jnp.max(worst)

    def cond(state):
        kb, worst = state
        return (kb >= 0) & (worst > -SKIP_BELOW)

    def body(state):
        kb, _ = state
        return kb - 1, step(kb, False)

    lax.while_loop(cond, body, (qi - 1, step(qi, True)))
    o_ref[...] = acc_ref[...].astype(o_ref.dtype)


def _attention(proj, B, S, W):
    T = proj.shape[0]
    n_heads = W // HEAD_DIM
    heads = min(4, n_heads)
    assert n_heads % heads == 0
    hw = heads * HEAD_DIM
    tq = _tile(S, 256)
    nq = S // tq
    q0, k0, v0 = 2 * W // hw, 3 * W // hw, 4 * W // hw
    kern = functools.partial(_attn_kernel, tq=tq, heads=heads)
    return pl.pallas_call(
        kern,
        grid=(B, n_heads // heads, nq),
        in_specs=[pl.BlockSpec((tq, hw), lambda b, g, i: (b * nq + i, q0 + g)),
                  pl.BlockSpec((S, hw), lambda b, g, i: (b, k0 + g)),
                  pl.BlockSpec((S, hw), lambda b, g, i: (b, v0 + g))],
        out_specs=pl.BlockSpec((tq, hw), lambda b, g, i: (b * nq + i, g)),
        out_shape=jax.ShapeDtypeStruct((T, W), BF16),
        scratch_shapes=[pltpu.VMEM((tq, hw), F32),
                        pltpu.VMEM((heads, tq, 1), F32)],
        compiler_params=_params(("parallel", "parallel", "arbitrary")),
        name="stick_breaking",
    )(proj, proj, proj)


def _merge_kernel(a_ref, b_ref, pa_ref, pb_ref, ga_ref, gb_ref, o_ref, pab_ref, pbb_ref):
    @pl.when(pl.program_id(1) == 0)
    def _():
        pab_ref[...] = pa_ref[...].astype(BF16)
        pbb_ref[...] = pb_ref[...].astype(BF16)

    cw = min(o_ref.shape[1], 256)
    for c in range(o_ref.shape[1] // cw):
        cols = slice(c * cw, (c + 1) * cw)
        ya = jnp.dot(a_ref[...], pab_ref[:, cols], preferred_element_type=F32)
        yb = jnp.dot(b_ref[...], pbb_ref[:, cols], preferred_element_type=F32)
        o_ref[:, cols] = (ga_ref[:, cols].astype(F32) * ya
                          + gb_ref[:, cols].astype(F32) * yb).astype(o_ref.dtype)


def _merge(a, b, proj, w_pa, w_pb, W, D):
    T = a.shape[0]
    tm = _tile(T, 1024)
    tn = _tile(W, 512)
    ga0 = 5 * W // tn
    gb0 = (5 * W + D) // tn
    return pl.pallas_call(
        _merge_kernel,
        grid=(D // tn, T // tm),
        in_specs=[pl.BlockSpec((tm, W), lambda j, i: (i, 0)),
                  pl.BlockSpec((tm, W), lambda j, i: (i, 0)),
                  pl.BlockSpec((W, tn), lambda j, i: (0, j)),
                  pl.BlockSpec((W, tn), lambda j, i: (0, j)),
                  pl.BlockSpec((tm, tn), lambda j, i: (i, ga0 + j)),
                  pl.BlockSpec((tm, tn), lambda j, i: (i, gb0 + j))],
        out_specs=pl.BlockSpec((tm, tn), lambda j, i: (i, j)),
        out_shape=jax.ShapeDtypeStruct((T, D), BF16),
        scratch_shapes=[pltpu.VMEM((W, tn), BF16), pltpu.VMEM((W, tn), BF16)],
        compiler_params=_params(("parallel", "arbitrary")),
        name="merge",
    )(a, b, w_pa, w_pb, proj, proj)


def _out_kernel(m_ref, w_ref, x_ref, o_ref, wb_ref):
    @pl.when(pl.program_id(1) == 0)
    def _():
        wb_ref[...] = w_ref[...].astype(BF16)

    cw = min(o_ref.shape[1], 256)
    for c in range(o_ref.shape[1] // cw):
        cols = slice(c * cw, (c + 1) * cw)
        o_ref[:, cols] = x_ref[:, cols] + jnp.dot(m_ref[...], wb_ref[:, cols],
                                                  preferred_element_type=F32)


def _out_proj(merged, w_out, x):
    T, D = x.shape
    tm = _tile(T, 1024)
    tn = _tile(D, 512)
    return pl.pallas_call(
        _out_kernel,
        grid=(D // tn, T // tm),
        in_specs=[pl.BlockSpec((tm, D), lambda j, i: (i, 0)),
                  pl.BlockSpec((D, tn), lambda j, i: (0, j)),
                  pl.BlockSpec((tm, tn), lambda j, i: (i, j))],
        out_specs=pl.BlockSpec((tm, tn), lambda j, i: (i, j)),
        out_shape=jax.ShapeDtypeStruct((T, D), F32),
        scratch_shapes=[pltpu.VMEM((D, tn), BF16)],
        compiler_params=_params(("parallel", "arbitrary")),
        name="out_proj",
    )(merged, w_out, x)


def _route_kernel(x_ref, g_ref, wr_ref, br_ref, xn_ref, e_ref, p_ref, c_ref):
    xn = _rms(x_ref[...], g_ref[...])
    xn_ref[...] = xn
    wr = wr_ref[...]
    wr_hi = wr.astype(BF16)
    wr_lo = (wr - wr_hi.astype(F32)).astype(BF16)
    xn_hi = xn.astype(BF16)
    xn_lo = (xn - xn_hi.astype(F32)).astype(BF16)
    logits = (jnp.dot(xn_hi, wr_hi, preferred_element_type=F32)
              + jnp.dot(xn_lo, wr_hi, preferred_element_type=F32)
              + jnp.dot(xn_hi, wr_lo, preferred_element_type=F32)) + br_ref[...]
    tm = logits.shape[0]
    lane = lax.broadcasted_iota(jnp.int32, logits.shape, 1)
    neg = -jnp.inf

    def top(vals):
        m = jnp.max(vals, axis=1, keepdims=True)
        idx = jnp.min(jnp.where(vals == m, lane, ROUTE_LANES), axis=1, keepdims=True)
        return m, idx

    is_group = lane < N_GROUPS_MOE
    gmax, grp = top(jnp.where(is_group, logits, neg))
    p_grp = 1.0 / jnp.sum(jnp.where(is_group, jnp.exp(logits - gmax), 0.0), axis=1, keepdims=True)
    first = N_GROUPS_MOE + grp * EXPERTS_PER_GROUP
    local = jnp.where((lane >= first) & (lane < first + EXPERTS_PER_GROUP), logits, neg)
    m1, i1 = top(local)
    m2, i2 = top(jnp.where(lane == i1, neg, local))
    e2 = jnp.exp(m2 - m1)
    w1 = p_grp / (1.0 + e2)
    w2 = p_grp * e2 / (1.0 + e2)
    ex1 = i1 - N_GROUPS_MOE
    ex2 = i2 - N_GROUPS_MOE

    hot1 = lane == ex1
    hot2 = lane == ex2
    chosen = (hot1 | hot2).astype(BF16)
    row = lax.broadcasted_iota(jnp.int32, (tm, tm), 0)
    col = lax.broadcasted_iota(jnp.int32, (tm, tm), 1)
    earlier = jnp.dot((col < row).astype(BF16), chosen, preferred_element_type=F32)
    r1 = jnp.sum(jnp.where(hot1, earlier, 0.0), axis=1, keepdims=True).astype(jnp.int32)
    r2 = jnp.sum(jnp.where(hot2, earlier, 0.0), axis=1, keepdims=True).astype(jnp.int32)
    counts = jnp.sum(chosen.astype(F32), axis=0, keepdims=True).astype(jnp.int32)

    e_ref[...] = jnp.where(lane == 0, ex1, jnp.where(lane == 1, ex2,
                           jnp.where(lane == 2, r1, jnp.where(lane == 3, r2, 0))))
    p_ref[...] = jnp.where(lane == 0, w1, jnp.where(lane == 1, w2, 0.0))
    c_ref[...] = jnp.broadcast_to(counts[None], c_ref.shape)


def _route(x1, g_ffn, w_group, b_group, w_router, b_router):
    T, D = x1.shape
    n_logit = w_group.shape[1] + w_router.shape[1]
    assert n_logit <= ROUTE_LANES
    wr = jnp.zeros((D, ROUTE_LANES), F32).at[:, :n_logit].set(jnp.concatenate([w_group, w_router], axis=1))
    br = jnp.zeros((1, ROUTE_LANES), F32).at[0, :n_logit].set(jnp.concatenate([b_group, b_router]))
    tm = _tile(T, 256)
    n_tiles = T // tm
    return tm, pl.pallas_call(
        _route_kernel,
        grid=(n_tiles,),
        in_specs=[pl.BlockSpec((tm, D), lambda i: (i, 0)),
                  pl.BlockSpec((1, D), lambda i: (0, 0)),
                  pl.BlockSpec((D, ROUTE_LANES), lambda i: (0, 0)),
                  pl.BlockSpec((1, ROUTE_LANES), lambda i: (0, 0))],
        out_specs=[pl.BlockSpec((tm, D), lambda i: (i, 0)),
                   pl.BlockSpec((tm, ROUTE_LANES), lambda i: (i, 0)),
                   pl.BlockSpec((tm, ROUTE_LANES), lambda i: (i, 0)),
                   pl.BlockSpec((1, 8, ROUTE_LANES), lambda i: (i, 0, 0))],
        out_shape=[jax.ShapeDtypeStruct((T, D), F32),
                   jax.ShapeDtypeStruct((T, ROUTE_LANES), jnp.int32),
                   jax.ShapeDtypeStruct((T, ROUTE_LANES), F32),
                   jax.ShapeDtypeStruct((n_tiles, 8, ROUTE_LANES), jnp.int32)],
        compiler_params=_params(("parallel",)),
        name="route",
    )(x1, g_ffn.reshape(1, D), wr, br)


def _dispatch_tables(experts, ranks, tile_counts, route_tm, n_exp, blk, n_blocks, n_split):
    T = experts.shape[0]
    i32 = jnp.int32
    tile_base = jnp.cumsum(tile_counts, axis=0) - tile_counts
    counts = jnp.sum(tile_counts, axis=0)
    padded = (counts + blk - 1) // blk * blk
    pad_end = jnp.cumsum(padded)
    pad_start = pad_end - padded
    base = jnp.repeat(tile_base + pad_start[None, :], route_tm, axis=0)
    hot = experts[:, :, None] == jnp.arange(n_exp, dtype=i32)[None, None, :]
    pos = (jnp.sum(jnp.where(hot, base[:, None, :], 0), axis=2) + ranks).astype(i32)
    token = jnp.broadcast_to(jnp.arange(T, dtype=i32)[:, None], pos.shape)
    pad_tok = jnp.arange(n_blocks * blk, dtype=i32) % T
    row_tok = pad_tok.at[pos.reshape(-1)].set(token.reshape(-1), unique_indices=True)

    n_used = pad_end[-1] // blk
    bstart = jnp.arange(n_blocks, dtype=i32) * blk
    be = jnp.minimum(jnp.sum((pad_end[None, :] <= bstart[:, None]).astype(i32), axis=1), n_exp - 1)
    be = jnp.where(jnp.arange(n_blocks) < n_used, be, be[n_used - 1])

    k = jnp.arange(n_blocks * n_split, dtype=i32)
    e_k = be[k // n_split]
    first_blk = pad_start[e_k] // blk
    n_blk = jnp.maximum(padded[e_k] // blk, 1)
    p = k - n_split * first_blk
    used = k < n_split * n_used
    idle = k - n_split * n_used
    item_blk = jnp.where(used, first_blk + p % n_blk, n_used + idle // n_split)
    item_split = jnp.where(used, p // n_blk, idle % n_split)
    item_wsplit = jnp.where(used, item_split, n_split - 1)
    item_first = used & (p % n_blk == 0)
    flags = used.astype(i32) + 2 * item_first.astype(i32)
    items = (item_blk.astype(i32), item_split.astype(i32), item_wsplit.astype(i32), e_k.astype(i32), flags)

    blk_id = jnp.arange(n_blocks, dtype=i32)
    blk_used = blk_id < n_used
    blk_first = blk_used & (blk_id == pad_start[be] // blk)
    block_flags = blk_used.astype(i32) + 2 * blk_first.astype(i32)
    return pos, row_tok, items, be.astype(i32), block_flags


def _dispatch_kernel(tok_hbm, x_hbm, o_ref, idx_smem, xbuf, idx_sem, row_sem, *, blk, n_blocks):
    i = pl.program_id(0)

    def idx_copy(b):
        return pltpu.make_async_copy(tok_hbm.at[pl.ds(pl.multiple_of(b * blk, blk), blk)],
                                     idx_smem.at[b % 2], idx_sem)

    def row_copy(tok, slot, r):
        return pltpu.make_async_copy(x_hbm.at[pl.ds(tok, 1), :], xbuf.at[slot, pl.ds(r, 1), :],
                                     row_sem.at[slot])

    def issue_rows(b):
        slot = b % 2

        def body(r, c):
            row_copy(idx_smem[slot, r], slot, r).start()
            return c
        lax.fori_loop(0, blk, body, 0, unroll=16)

    def wait_rows(slot):
        pltpu.make_async_copy(x_hbm.at[pl.ds(0, blk), :], xbuf.at[slot], row_sem.at[slot]).wait()

    @pl.when(i == 0)
    def _():
        idx_copy(0).start()
        idx_copy(0).wait()
        issue_rows(0)
        if n_blocks > 1:
            idx_copy(1).start()

    @pl.when(i + 1 < n_blocks)
    def _():
        idx_copy(i + 1).wait()
        issue_rows(i + 1)

    @pl.when(i + 2 < n_blocks)
    def _():
        idx_copy(i + 2).start()

    slot = i % 2
    wait_rows(slot)
    o_ref[...] = xbuf[slot].astype(o_ref.dtype)


def _dispatch(xn, row_tok, blk, n_blocks):
    T, D = xn.shape
    kern = functools.partial(_dispatch_kernel, blk=blk, n_blocks=n_blocks)
    return pl.pallas_call(
        kern,
        grid=(n_blocks,),
        in_specs=[pl.BlockSpec(memory_space=pl.ANY),
                  pl.BlockSpec(memory_space=pl.ANY)],
        out_specs=pl.BlockSpec((blk, D), lambda i: (i, 0)),
        out_shape=jax.ShapeDtypeStruct((n_blocks * blk, D), BF16),
        scratch_shapes=[pltpu.SMEM((2, blk), jnp.int32),
                        pltpu.VMEM((2, blk, D), F32),
                        pltpu.SemaphoreType.DMA(()),
                        pltpu.SemaphoreType.DMA((2,))],
        compiler_params=_params(("arbitrary",)),
        name="moe_dispatch",
    )(row_tok, xn)


def _moe_up_kernel(blk_ref, split_ref, wsplit_ref, e_ref, flag_ref,
                   x_ref, wg_ref, wu_ref, o_ref, wgb_ref, wub_ref):
    flag = flag_ref[pl.program_id(0)]
    cw = min(o_ref.shape[1], 256)

    def project(convert):
        for c in range(o_ref.shape[1] // cw):
            cols = slice(c * cw, (c + 1) * cw)
            if convert:
                wgb_ref[:, cols] = wg_ref[:, cols].astype(BF16)
                wub_ref[:, cols] = wu_ref[:, cols].astype(BF16)
            g = jnp.dot(x_ref[...], wgb_ref[:, cols], preferred_element_type=F32)
            u = jnp.dot(x_ref[...], wub_ref[:, cols], preferred_element_type=F32)
            o_ref[:, cols] = (g * _sigmoid(g) * u).astype(o_ref.dtype)

    @pl.when(flag == 3)
    def _():
        project(True)

    @pl.when(flag == 1)
    def _():
        project(False)

    @pl.when(flag == 0)
    def _():
        o_ref[...] = jnp.zeros_like(o_ref)


def _moe_up(xs, items, w_gate, w_up, blk, n_split):
    n_rows, D = xs.shape
    hid = w_gate.shape[2]
    th = hid // n_split
    n_items = items[0].shape[0]
    grid_spec = pltpu.PrefetchScalarGridSpec(
        num_scalar_prefetch=5,
        grid=(n_items,),
        in_specs=[pl.BlockSpec((blk, D), lambda k, b, s, ws, e, f: (b[k], 0)),
                  pl.BlockSpec((None, D, th), lambda k, b, s, ws, e, f: (e[k], 0, ws[k])),
                  pl.BlockSpec((None, D, th), lambda k, b, s, ws, e, f: (e[k], 0, ws[k]))],
        out_specs=pl.BlockSpec((blk, th), lambda k, b, s, ws, e, f: (b[k], s[k])),
        scratch_shapes=[pltpu.VMEM((D, th), BF16), pltpu.VMEM((D, th), BF16)],
    )
    return pl.pallas_call(
        _moe_up_kernel,
        grid_spec=grid_spec,
        out_shape=jax.ShapeDtypeStruct((n_rows, hid), BF16),
        compiler_params=_params(("arbitrary",)),
        name="moe_up",
    )(*items, xs, w_gate, w_up)


def _moe_down_kernel(e_ref, flag_ref, h_ref, wd_ref, o_ref, wdb_ref):
    flag = flag_ref[pl.program_id(0)]
    cw = min(o_ref.shape[1], 256)

    def project(convert):
        for c in range(o_ref.shape[1] // cw):
            cols = slice(c * cw, (c + 1) * cw)
            if convert:
                wdb_ref[:, cols] = wd_ref[:, cols].astype(BF16)
            o_ref[:, cols] = jnp.dot(h_ref[...], wdb_ref[:, cols], preferred_element_type=F32)

    @pl.when(flag == 3)
    def _():
        project(True)

    @pl.when(flag == 1)
    def _():
        project(False)

    @pl.when(flag == 0)
    def _():
        o_ref[...] = jnp.zeros_like(o_ref)


def _moe_down(hid, block_expert, block_flags, w_down, blk):
    n_rows, H = hid.shape
    D = w_down.shape[2]
    grid_spec = pltpu.PrefetchScalarGridSpec(
        num_scalar_prefetch=2,
        grid=(n_rows // blk,),
        in_specs=[pl.BlockSpec((blk, H), lambda i, e, f: (i, 0)),
                  pl.BlockSpec((None, H, D), lambda i, e, f: (e[i], 0, 0))],
        out_specs=pl.BlockSpec((blk, D), lambda i, e, f: (i, 0)),
        scratch_shapes=[pltpu.VMEM((H, D), BF16)],
    )
    return pl.pallas_call(
        _moe_down_kernel,
        grid_spec=grid_spec,
        out_shape=jax.ShapeDtypeStruct((n_rows, D), F32),
        compiler_params=_params(("arbitrary",)),
        name="moe_down",
    )(block_expert, block_flags, hid, w_down)


def _combine_kernel(pos_hbm, y_hbm, x_ref, w_ref, g_ref, o_ref,
                    idx_smem, ybuf, idx_sem, row_sem, *, tm, n_tiles, final_norm):
    i = pl.program_id(0)
    n_idx = TOP_K * tm

    def idx_copy(t):
        return pltpu.make_async_copy(pos_hbm.at[pl.ds(pl.multiple_of(t * n_idx, n_idx), n_idx)],
                                     idx_smem.at[t % 2], idx_sem)

    def row_copy(src, slot, dst):
        return pltpu.make_async_copy(y_hbm.at[pl.ds(src, 1), :], ybuf.at[slot, pl.ds(dst, 1), :],
                                     row_sem.at[slot])

    def issue_rows(t):
        slot = t % 2

        def body(r, c):
            for k in range(TOP_K):
                row_copy(idx_smem[slot, TOP_K * r + k], slot, k * tm + r).start()
            return c
        lax.fori_loop(0, tm, body, 0, unroll=8)

    def wait_rows(slot):
        pltpu.make_async_copy(y_hbm.at[pl.ds(0, n_idx), :], ybuf.at[slot], row_sem.at[slot]).wait()

    @pl.when(i == 0)
    def _():
        idx_copy(0).start()
        idx_copy(0).wait()
        issue_rows(0)
        if n_tiles > 1:
            idx_copy(1).start()

    @pl.when(i + 1 < n_tiles)
    def _():
        idx_copy(i + 1).wait()
        issue_rows(i + 1)

    @pl.when(i + 2 < n_tiles)
    def _():
        idx_copy(i + 2).start()

    slot = i % 2
    wait_rows(slot)
    w = w_ref[...]
    out = x_ref[...] + w[:, 0:1] * ybuf[slot, 0:tm, :] + w[:, 1:2] * ybuf[slot, tm:2 * tm, :]
    if final_norm:
        out = _rms(out, g_ref[...])
    o_ref[...] = out


def _combine(x1, y, pos, w_assign, g_final, final_norm):
    T, D = x1.shape
    tm = _tile(T, 256)
    n_tiles = T // tm
    kern = functools.partial(_combine_kernel, tm=tm, n_tiles=n_tiles, final_norm=final_norm)
    return pl.pallas_call(
        kern,
        grid=(n_tiles,),
        in_specs=[pl.BlockSpec(memory_space=pl.ANY),
                  pl.BlockSpec(memory_space=pl.ANY),
                  pl.BlockSpec((tm, D), lambda i: (i, 0)),
                  pl.BlockSpec((tm, TOP_K), lambda i: (i, 0)),
                  pl.BlockSpec((1, D), lambda i: (0, 0))],
        out_specs=pl.BlockSpec((tm, D), lambda i: (i, 0)),
        out_shape=jax.ShapeDtypeStruct((T, D), F32),
        scratch_shapes=[pltpu.SMEM((2, TOP_K * tm), jnp.int32),
                        pltpu.VMEM((2, TOP_K * tm, D), F32),
                        pltpu.SemaphoreType.DMA(()),
                        pltpu.SemaphoreType.DMA((2,))],
        compiler_params=_params(("arbitrary",)),
        name="moe_combine",
    )(pos.reshape(-1), y, x1, w_assign, g_final.reshape(1, D))


def _layer(x, B, S, p, g_final, final_norm):
    T, D = x.shape
    W = D // 2
    xn = _norm_bf16(x, p["g_mix"])
    proj = _in_proj(xn, p["w_in"], p["b_gate"], W)
    a = _sgu(proj, p["g_sgu"], p["w_sgu"], p["b_sgu"], W)
    b = _attention(proj, B, S, W)
    merged = _merge(a, b, proj, p["w_proj_a"], p["w_proj_b"], W, D)
    x1 = _out_proj(merged, p["w_out"], x)

    route_tm, (xn2, e_out, p_out, tile_counts) = _route(
        x1, p["g_ffn"], p["w_group"], p["b_group"], p["w_router"], p["b_router"])
    n_exp = p["w_gate"].shape[0]
    blk = 256
    n_split = 2
    n_blocks = T * TOP_K // blk + n_exp
    pos, row_tok, items, block_expert, block_flags = _dispatch_tables(
        e_out[:, :TOP_K], e_out[:, TOP_K:2 * TOP_K], tile_counts[:, 0, :n_exp],
        route_tm, n_exp, blk, n_blocks, n_split)
    xs = _dispatch(xn2, row_tok, blk, n_blocks)
    hid = _moe_up(xs, items, p["w_gate"], p["w_up"], blk, n_split)
    y = _moe_down(hid, block_expert, block_flags, p["w_down"], blk)
    return _combine(x1, y, pos, p_out[:, :TOP_K], g_final, final_norm)


def kernel(x, g_mix, w_in, g_sgu, w_sgu, b_sgu, b_gate, w_proj_a, w_proj_b, w_out,
           g_ffn, w_group, b_group, w_router, b_router, w_gate, w_up, w_down, g_final):
    B, S, D = x.shape
    stacked = dict(g_mix=g_mix, w_in=w_in, g_sgu=g_sgu, w_sgu=w_sgu, b_sgu=b_sgu, b_gate=b_gate,
                   w_proj_a=w_proj_a, w_proj_b=w_proj_b, w_out=w_out, g_ffn=g_ffn,
                   w_group=w_group, b_group=b_group, w_router=w_router, b_router=b_router,
                   w_gate=w_gate, w_up=w_up, w_down=w_down)
    depth = w_in.shape[0]
    xf = x.reshape(B * S, D)
    for l in range(depth):
        p = {name: val[l] for name, val in stacked.items()}
        xf = _layer(xf, B, S, p, g_final, final_norm=(l == depth - 1))
    return xf.reshape(B, S, D)
```

```python
import functools

import jax
import jax.numpy as jnp
from jax import lax
from jax.experimental import pallas as pl
from jax.experimental.pallas import tpu as pltpu

F32 = jnp.float32
BF16 = jnp.bfloat16

EPS = 1e-6
CHUNK = 64
SGU_BLOCK = 128
SGU_GROUPS = 8
HEAD_DIM = 128
N_GROUPS_MOE = 4
EXPERTS_PER_GROUP = 8
TOP_K = 2
ROUTE_LANES = 128
USED, FIRST, HAS_NEXT = 1, 2, 4

SKIP_BELOW = 120.0

VMEM_LIMIT = 56 * 2**20


def _tile(n, pref):
    t = min(n, pref)
    assert n % t == 0, (n, pref)
    return t


def _params(sem):
    return pltpu.CompilerParams(dimension_semantics=sem, vmem_limit_bytes=VMEM_LIMIT)


def _sigmoid(x):
    return 1.0 / (1.0 + jnp.exp(-x))


def _rms(x, g):
    return x * lax.rsqrt(jnp.mean(x * x, axis=-1, keepdims=True) + EPS) * g


def _norm_kernel(x_ref, g_ref, o_ref):
    o_ref[...] = _rms(x_ref[...], g_ref[...]).astype(o_ref.dtype)


def _norm_bf16(x, g):
    T, D = x.shape
    tm = _tile(T, 512)
    return pl.pallas_call(
        _norm_kernel,
        grid=(T // tm,),
        in_specs=[pl.BlockSpec((tm, D), lambda i: (i, 0)),
                  pl.BlockSpec((1, D), lambda i: (0, 0))],
        out_specs=pl.BlockSpec((tm, D), lambda i: (i, 0)),
        out_shape=jax.ShapeDtypeStruct((T, D), BF16),
        compiler_params=_params(("parallel",)),
        name="norm_bf16",
    )(x, g.reshape(1, D))


def _load_column_tile(w_hbm, stage_ref, wb_ref, sem, j, n_tiles):
    tn = stage_ref.shape[1]

    def copy(t):
        return pltpu.make_async_copy(w_hbm.at[:, pl.ds(pl.multiple_of(t * tn, tn), tn)], stage_ref, sem)

    @pl.when(j == 0)
    def _():
        copy(0).start()

    copy(j).wait()
    wb_ref[...] = stage_ref[...].astype(BF16)

    @pl.when(j + 1 < n_tiles)
    def _():
        copy(j + 1).start()


def _proj_kernel(x_ref, w_hbm, c_ref, o_ref, stage_ref, wb_ref, sem, *, n_chunks):
    @pl.when(pl.program_id(1) == 0)
    def _():
        _load_column_tile(w_hbm, stage_ref, wb_ref, sem, pl.program_id(0), pl.num_programs(0))

    cw = o_ref.shape[1] // n_chunks
    for c in range(n_chunks):
        cols = slice(c * cw, (c + 1) * cw)
        y = jnp.dot(x_ref[...], wb_ref[:, cols], preferred_element_type=F32)
        a0, a1 = c_ref[0:1, cols], c_ref[1:2, cols]
        b0, b1, b3 = c_ref[2:3, cols], c_ref[3:4, cols], c_ref[4:5, cols]
        o_ref[:, cols] = ((a0 + a1 * y) * _sigmoid(b0 + y * (b1 + b3 * (y * y)))).astype(o_ref.dtype)


def _proj_epilogue_coefficients(b_gate, W):
    c2 = 2.0 * 0.7978845608028654
    n_gate = b_gate.size
    seg = lambda v, n: jnp.full((n,), v, F32)
    a0 = jnp.concatenate([seg(0.0, 5 * W), seg(1.0, n_gate)])
    a1 = jnp.concatenate([seg(1.0, 2 * W), seg(HEAD_DIM ** -0.5, W), seg(1.0, 2 * W), seg(0.0, n_gate)])
    b0 = jnp.concatenate([seg(0.0, 2 * W), seg(1e30, 3 * W), b_gate.reshape(-1).astype(F32)])
    b1 = jnp.concatenate([seg(c2, 2 * W), seg(0.0, 3 * W), seg(1.0, n_gate)])
    b3 = jnp.concatenate([seg(c2 * 0.044715, 2 * W), seg(0.0, 3 * W + n_gate)])
    zero = jnp.zeros_like(a0)
    return jnp.stack([a0, a1, b0, b1, b3, zero, zero, zero])


def _in_proj(xn, w_in, b_gate, W):
    T, D = xn.shape
    N = w_in.shape[1]
    tm = _tile(T, 1024)
    tn = _tile(W, 1024)
    coef = _proj_epilogue_coefficients(b_gate, W)
    assert coef.shape == (8, N)
    kern = functools.partial(_proj_kernel, n_chunks=max(tn // 256, 1))
    return pl.pallas_call(
        kern,
        grid=(N // tn, T // tm),
        in_specs=[pl.BlockSpec((tm, D), lambda j, i: (i, 0)),
                  pl.BlockSpec(memory_space=pl.ANY),
                  pl.BlockSpec((8, tn), lambda j, i: (0, j))],
        out_specs=pl.BlockSpec((tm, tn), lambda j, i: (i, j)),
        out_shape=jax.ShapeDtypeStruct((T, N), BF16),
        scratch_shapes=[pltpu.VMEM((D, tn), F32), pltpu.VMEM((D, tn), BF16),
                        pltpu.SemaphoreType.DMA(())],
        compiler_params=_params(("arbitrary", "arbitrary")),
        name="in_proj",
    )(xn, w_in, coef)


def _sgu_kernel(u_ref, v_ref, g_ref, w_ref, bt_ref, o_ref, *, n_sub, gd):
    v = v_ref[...].astype(F32)
    vn = _rms(v, g_ref[...]).astype(BF16)
    t_chunk = lax.broadcasted_iota(jnp.int32, (SGU_BLOCK, SGU_BLOCK), 0) // CHUNK
    s_chunk = lax.broadcasted_iota(jnp.int32, (SGU_BLOCK, SGU_BLOCK), 1) // CHUNK
    causal = s_chunk <= t_chunk
    for g in range(SGU_GROUPS):
        w = jnp.where(causal, w_ref[g], 0.0).astype(BF16)
        bias = bt_ref[:, g:g + 1]
        cols = slice(g * gd, (g + 1) * gd)
        for n in range(n_sub):
            rows = slice(n * SGU_BLOCK, (n + 1) * SGU_BLOCK)
            mixed = jnp.dot(w, vn[rows, cols], preferred_element_type=F32) + bias
            o_ref[rows, cols] = (u_ref[rows, cols].astype(F32) * mixed).astype(o_ref.dtype)


def _sgu(proj, g_sgu, w_sgu, b_sgu, W):
    T = proj.shape[0]
    tm = _tile(T, 2 * SGU_BLOCK)
    gd = W // SGU_GROUPS
    kern = functools.partial(_sgu_kernel, n_sub=tm // SGU_BLOCK, gd=gd)
    return pl.pallas_call(
        kern,
        grid=(T // tm,),
        in_specs=[pl.BlockSpec((tm, W), lambda i: (i, 0)),
                  pl.BlockSpec((tm, W), lambda i: (i, 1)),
                  pl.BlockSpec((1, W), lambda i: (0, 0)),
                  pl.BlockSpec((SGU_GROUPS, SGU_BLOCK, SGU_BLOCK), lambda i: (0, 0, 0)),
                  pl.BlockSpec((SGU_BLOCK, SGU_GROUPS), lambda i: (0, 0))],
        out_specs=pl.BlockSpec((tm, W), lambda i: (i, 0)),
        out_shape=jax.ShapeDtypeStruct((T, W), BF16),
        compiler_params=_params(("parallel",)),
        name="sgu",
    )(proj, proj, g_sgu.reshape(1, W), w_sgu, b_sgu.T)


def _attn_kernel(q_ref, k_ref, v_ref, o_ref, acc_ref, carry_ref, *, tq, heads):
    qi = pl.program_id(2)
    jj = lax.broadcasted_iota(jnp.int32, (tq, tq), 0)
    ss = lax.broadcasted_iota(jnp.int32, (tq, tq), 1)
    tri = (jj >= ss).astype(BF16)
    before = ss < jj

    def step(kb, diagonal):
        rows = pl.ds(pl.multiple_of(kb * tq, tq), tq)
        worst = None
        for h in range(heads):
            cols = slice(h * HEAD_DIM, (h + 1) * HEAD_DIM)
            z = lax.dot_general(q_ref[:, cols], k_ref[rows, cols], (((1,), (1,)), ((), ())),
                                preferred_element_type=F32)
            lg = -(jnp.maximum(z, 0.0) + jnp.log(1.0 + jnp.exp(-jnp.abs(z))))
            if diagonal:
                lg = jnp.where(before, lg, 0.0)
            hi = lg.astype(BF16)
            lo = (lg - hi.astype(F32)).astype(BF16)
            incl = (jnp.dot(hi, tri, preferred_element_type=F32)
                    + jnp.dot(lo, tri, preferred_element_type=F32))
            if diagonal:
                a = jnp.where(before, jnp.exp(z + incl), 0.0)
                carry = incl[:, 0:1]
            else:
                a = jnp.exp(z + incl + carry_ref[h])
                carry = carry_ref[h] + incl[:, 0:1]
            pv = jnp.dot(a.astype(BF16), v_ref[rows, cols], preferred_element_type=F32)
            if diagonal:
                acc_ref[:, cols] = pv
            else:
                acc_ref[:, cols] += pv
            carry_ref[h] = carry
            worst = carry if worst is None else jnp.maximum(worst, carry)
        return jnp.max(worst)

    def cond(state):
        kb, worst = state
        return (kb >= 0) & (worst > -SKIP_BELOW)

    def body(state):
        kb, _ = state
        return kb - 1, step(kb, False)

    lax.while_loop(cond, body, (qi - 1, step(qi, True)))
    o_ref[...] = acc_ref[...].astype(o_ref.dtype)


def _attention(proj, B, S, W):
    T = proj.shape[0]
    n_heads = W // HEAD_DIM
    heads = min(4, n_heads)
    assert n_heads % heads == 0
    hw = heads * HEAD_DIM
    tq = _tile(S, 256)
    nq = S // tq
    q0, k0, v0 = 2 * W // hw, 3 * W // hw, 4 * W // hw
    kern = functools.partial(_attn_kernel, tq=tq, heads=heads)
    return pl.pallas_call(
        kern,
        grid=(B, n_heads // heads, nq),
        in_specs=[pl.BlockSpec((tq, hw), lambda b, g, i: (b * nq + i, q0 + g)),
                  pl.BlockSpec((S, hw), lambda b, g, i: (b, k0 + g)),
                  pl.BlockSpec((S, hw), lambda b, g, i: (b, v0 + g))],
        out_specs=pl.BlockSpec((tq, hw), lambda b, g, i: (b * nq + i, g)),
        out_shape=jax.ShapeDtypeStruct((T, W), BF16),
        scratch_shapes=[pltpu.VMEM((tq, hw), F32),
                        pltpu.VMEM((heads, tq, 1), F32)],
        compiler_params=_params(("parallel", "parallel", "arbitrary")),
        name="stick_breaking",
    )(proj, proj, proj)


def _merge_kernel(a_ref, b_ref, pa_hbm, pb_hbm, ga_ref, gb_ref, o_ref,
                  sa_ref, sb_ref, pab_ref, pbb_ref, sem):
    @pl.when(pl.program_id(1) == 0)
    def _():
        j, n = pl.program_id(0), pl.num_programs(0)
        _load_column_tile(pa_hbm, sa_ref, pab_ref, sem.at[0], j, n)
        _load_column_tile(pb_hbm, sb_ref, pbb_ref, sem.at[1], j, n)

    cw = min(o_ref.shape[1], 256)
    for c in range(o_ref.shape[1] // cw):
        cols = slice(c * cw, (c + 1) * cw)
        ya = jnp.dot(a_ref[...], pab_ref[:, cols], preferred_element_type=F32)
        yb = jnp.dot(b_ref[...], pbb_ref[:, cols], preferred_element_type=F32)
        o_ref[:, cols] = (ga_ref[:, cols].astype(F32) * ya
                          + gb_ref[:, cols].astype(F32) * yb).astype(o_ref.dtype)


def _merge(a, b, proj, w_pa, w_pb, W, D):
    T = a.shape[0]
    tm = _tile(T, 512)
    tn = _tile(W, 1024)
    ga0 = 5 * W // tn
    gb0 = (5 * W + D) // tn
    return pl.pallas_call(
        _merge_kernel,
        grid=(D // tn, T // tm),
        in_specs=[pl.BlockSpec((tm, W), lambda j, i: (i, 0)),
                  pl.BlockSpec((tm, W), lambda j, i: (i, 0)),
                  pl.BlockSpec(memory_space=pl.ANY),
                  pl.BlockSpec(memory_space=pl.ANY),
                  pl.BlockSpec((tm, tn), lambda j, i: (i, ga0 + j)),
                  pl.BlockSpec((tm, tn), lambda j, i: (i, gb0 + j))],
        out_specs=pl.BlockSpec((tm, tn), lambda j, i: (i, j)),
        out_shape=jax.ShapeDtypeStruct((T, D), BF16),
        scratch_shapes=[pltpu.VMEM((W, tn), F32), pltpu.VMEM((W, tn), F32),
                        pltpu.VMEM((W, tn), BF16), pltpu.VMEM((W, tn), BF16),
                        pltpu.SemaphoreType.DMA((2,))],
        compiler_params=_params(("arbitrary", "arbitrary")),
        name="merge",
    )(a, b, w_pa, w_pb, proj, proj)


def _out_kernel(m_ref, w_hbm, x_ref, o_ref, stage_ref, wb_ref, sem):
    @pl.when(pl.program_id(1) == 0)
    def _():
        _load_column_tile(w_hbm, stage_ref, wb_ref, sem, pl.program_id(0), pl.num_programs(0))

    cw = min(o_ref.shape[1], 256)
    for c in range(o_ref.shape[1] // cw):
        cols = slice(c * cw, (c + 1) * cw)
        o_ref[:, cols] = x_ref[:, cols] + jnp.dot(m_ref[...], wb_ref[:, cols],
                                                  preferred_element_type=F32)


def _out_proj(merged, w_out, x):
    T, D = x.shape
    tm = _tile(T, 512)
    tn = _tile(D, 1024)
    return pl.pallas_call(
        _out_kernel,
        grid=(D // tn, T // tm),
        in_specs=[pl.BlockSpec((tm, D), lambda j, i: (i, 0)),
                  pl.BlockSpec(memory_space=pl.ANY),
                  pl.BlockSpec((tm, tn), lambda j, i: (i, j))],
        out_specs=pl.BlockSpec((tm, tn), lambda j, i: (i, j)),
        out_shape=jax.ShapeDtypeStruct((T, D), F32),
        scratch_shapes=[pltpu.VMEM((D, tn), F32), pltpu.VMEM((D, tn), BF16),
                        pltpu.SemaphoreType.DMA(())],
        compiler_params=_params(("arbitrary", "arbitrary")),
        name="out_proj",
    )(merged, w_out, x)


def _route_kernel(x_ref, g_ref, wr_ref, br_ref, xn_ref, e_ref, p_ref, c_ref):
    xn = _rms(x_ref[...], g_ref[...])
    xn_ref[...] = xn
    wr = wr_ref[...]
    wr_hi = wr.astype(BF16)
    wr_lo = (wr - wr_hi.astype(F32)).astype(BF16)
    xn_hi = xn.astype(BF16)
    xn_lo = (xn - xn_hi.astype(F32)).astype(BF16)
    logits = (jnp.dot(xn_hi, wr_hi, preferred_element_type=F32)
              + jnp.dot(xn_lo, wr_hi, preferred_element_type=F32)
              + jnp.dot(xn_hi, wr_lo, preferred_element_type=F32)) + br_ref[...]
    tm = logits.shape[0]
    lane = lax.broadcasted_iota(jnp.int32, logits.shape, 1)
    neg = -jnp.inf

    def top(vals):
        m = jnp.max(vals, axis=1, keepdims=True)
        idx = jnp.min(jnp.where(vals == m, lane, ROUTE_LANES), axis=1, keepdims=True)
        return m, idx

    is_group = lane < N_GROUPS_MOE
    gmax, grp = top(jnp.where(is_group, logits, neg))
    p_grp = 1.0 / jnp.sum(jnp.where(is_group, jnp.exp(logits - gmax), 0.0), axis=1, keepdims=True)
    first = N_GROUPS_MOE + grp * EXPERTS_PER_GROUP
    local = jnp.where((lane >= first) & (lane < first + EXPERTS_PER_GROUP), logits, neg)
    m1, i1 = top(local)
    m2, i2 = top(jnp.where(lane == i1, neg, local))
    e2 = jnp.exp(m2 - m1)
    w1 = p_grp / (1.0 + e2)
    w2 = p_grp * e2 / (1.0 + e2)
    ex1 = i1 - N_GROUPS_MOE
    ex2 = i2 - N_GROUPS_MOE

    hot1 = lane == ex1
    hot2 = lane == ex2
    chosen = (hot1 | hot2).astype(BF16)
    row = lax.broadcasted_iota(jnp.int32, (tm, tm), 0)
    col = lax.broadcasted_iota(jnp.int32, (tm, tm), 1)
    earlier = jnp.dot((col < row).astype(BF16), chosen, preferred_element_type=F32)
    r1 = jnp.sum(jnp.where(hot1, earlier, 0.0), axis=1, keepdims=True).astype(jnp.int32)
    r2 = jnp.sum(jnp.where(hot2, earlier, 0.0), axis=1, keepdims=True).astype(jnp.int32)
    counts = jnp.sum(chosen.astype(F32), axis=0, keepdims=True).astype(jnp.int32)

    e_ref[...] = jnp.where(lane == 0, ex1, jnp.where(lane == 1, ex2,
                           jnp.where(lane == 2, r1, jnp.where(lane == 3, r2, 0))))
    p_ref[...] = jnp.where(lane == 0, w1, jnp.where(lane == 1, w2, 0.0))
    c_ref[...] = jnp.broadcast_to(counts[None], c_ref.shape)


def _route(x1, g_ffn, w_group, b_group, w_router, b_router):
    T, D = x1.shape
    n_logit = w_group.shape[1] + w_router.shape[1]
    assert n_logit <= ROUTE_LANES
    wr = jnp.zeros((D, ROUTE_LANES), F32).at[:, :n_logit].set(jnp.concatenate([w_group, w_router], axis=1))
    br = jnp.zeros((1, ROUTE_LANES), F32).at[0, :n_logit].set(jnp.concatenate([b_group, b_router]))
    tm = _tile(T, 256)
    n_tiles = T // tm
    return tm, pl.pallas_call(
        _route_kernel,
        grid=(n_tiles,),
        in_specs=[pl.BlockSpec((tm, D), lambda i: (i, 0)),
                  pl.BlockSpec((1, D), lambda i: (0, 0)),
                  pl.BlockSpec((D, ROUTE_LANES), lambda i: (0, 0)),
                  pl.BlockSpec((1, ROUTE_LANES), lambda i: (0, 0))],
        out_specs=[pl.BlockSpec((tm, D), lambda i: (i, 0)),
                   pl.BlockSpec((tm, ROUTE_LANES), lambda i: (i, 0)),
                   pl.BlockSpec((tm, ROUTE_LANES), lambda i: (i, 0)),
                   pl.BlockSpec((1, 8, ROUTE_LANES), lambda i: (i, 0, 0))],
        out_shape=[jax.ShapeDtypeStruct((T, D), F32),
                   jax.ShapeDtypeStruct((T, ROUTE_LANES), jnp.int32),
                   jax.ShapeDtypeStruct((T, ROUTE_LANES), F32),
                   jax.ShapeDtypeStruct((n_tiles, 8, ROUTE_LANES), jnp.int32)],
        compiler_params=_params(("parallel",)),
        name="route",
    )(x1, g_ffn.reshape(1, D), wr, br)


def _dispatch_tables(experts, ranks, tile_counts, route_tm, n_exp, blk, n_blocks, n_split):
    T = experts.shape[0]
    i32 = jnp.int32
    tile_base = jnp.cumsum(tile_counts, axis=0) - tile_counts
    counts = jnp.sum(tile_counts, axis=0)
    padded = (counts + blk - 1) // blk * blk
    pad_end = jnp.cumsum(padded)
    pad_start = pad_end - padded
    base = jnp.repeat(tile_base + pad_start[None, :], route_tm, axis=0)
    hot = experts[:, :, None] == jnp.arange(n_exp, dtype=i32)[None, None, :]
    pos = (jnp.sum(jnp.where(hot, base[:, None, :], 0), axis=2) + ranks).astype(i32)
    token = jnp.broadcast_to(jnp.arange(T, dtype=i32)[:, None], pos.shape)
    pad_tok = jnp.arange(n_blocks * blk, dtype=i32) % T
    row_tok = pad_tok.at[pos.reshape(-1)].set(token.reshape(-1), unique_indices=True)

    n_used = pad_end[-1] // blk
    bstart = jnp.arange(n_blocks, dtype=i32) * blk
    be = jnp.minimum(jnp.sum((pad_end[None, :] <= bstart[:, None]).astype(i32), axis=1), n_exp - 1)
    be = jnp.where(jnp.arange(n_blocks) < n_used, be, be[n_used - 1])

    k = jnp.arange(n_blocks * n_split, dtype=i32)
    e_k = be[k // n_split]
    first_blk = pad_start[e_k] // blk
    n_blk = jnp.maximum(padded[e_k] // blk, 1)
    p = k - n_split * first_blk
    used = k < n_split * n_used
    idle = k - n_split * n_used
    item_blk = jnp.where(used, first_blk + p % n_blk, n_used + idle // n_split).astype(i32)
    item_split = jnp.where(used, p // n_blk, idle % n_split).astype(i32)
    item_first = used & (p % n_blk == 0)
    nxt = k + n_blk
    item_has_next = item_first & (nxt < n_split * n_used)
    nxt = jnp.minimum(nxt, n_blocks * n_split - 1)
    flags = (USED * used + FIRST * item_first + HAS_NEXT * item_has_next).astype(i32)
    items = (item_blk, item_split, e_k.astype(i32), e_k[nxt].astype(i32), item_split[nxt], flags)

    blk_id = jnp.arange(n_blocks, dtype=i32)
    blk_used = blk_id < n_used
    blk_first = blk_used & (blk_id == pad_start[be] // blk)
    nxt_blk = blk_id + jnp.maximum(padded[be] // blk, 1)
    blk_has_next = blk_first & (nxt_blk < n_used)
    nxt_blk = jnp.minimum(nxt_blk, n_blocks - 1)
    block_flags = (USED * blk_used + FIRST * blk_first + HAS_NEXT * blk_has_next).astype(i32)
    return pos, row_tok, items, (be.astype(i32), be[nxt_blk].astype(i32), block_flags)


def _dispatch_kernel(tok_hbm, x_hbm, o_ref, idx_smem, xbuf, idx_sem, row_sem, *, blk, n_blocks):
    i = pl.program_id(0)

    def idx_copy(b):
        return pltpu.make_async_copy(tok_hbm.at[pl.ds(pl.multiple_of(b * blk, blk), blk)],
                                     idx_smem.at[b % 2], idx_sem)

    def row_copy(tok, slot, r):
        return pltpu.make_async_copy(x_hbm.at[pl.ds(tok, 1), :], xbuf.at[slot, pl.ds(r, 1), :],
                                     row_sem.at[slot])

    def issue_rows(b):
        slot = b % 2

        def body(r, c):
            row_copy(idx_smem[slot, r], slot, r).start()
            return c
        lax.fori_loop(0, blk, body, 0, unroll=16)

    def wait_rows(slot):
        pltpu.make_async_copy(x_hbm.at[pl.ds(0, blk), :], xbuf.at[slot], row_sem.at[slot]).wait()

    @pl.when(i == 0)
    def _():
        idx_copy(0).start()
        idx_copy(0).wait()
        issue_rows(0)
        if n_blocks > 1:
            idx_copy(1).start()

    @pl.when(i + 1 < n_blocks)
    def _():
        idx_copy(i + 1).wait()
        issue_rows(i + 1)

    @pl.when(i + 2 < n_blocks)
    def _():
        idx_copy(i + 2).start()

    slot = i % 2
    wait_rows(slot)
    o_ref[...] = xbuf[slot].astype(o_ref.dtype)


def _dispatch(xn, row_tok, blk, n_blocks):
    T, D = xn.shape
    kern = functools.partial(_dispatch_kernel, blk=blk, n_blocks=n_blocks)
    return pl.pallas_call(
        kern,
        grid=(n_blocks,),
        in_specs=[pl.BlockSpec(memory_space=pl.ANY),
                  pl.BlockSpec(memory_space=pl.ANY)],
        out_specs=pl.BlockSpec((blk, D), lambda i: (i, 0)),
        out_shape=jax.ShapeDtypeStruct((n_blocks * blk, D), BF16),
        scratch_shapes=[pltpu.SMEM((2, blk), jnp.int32),
                        pltpu.VMEM((2, blk, D), F32),
                        pltpu.SemaphoreType.DMA(()),
                        pltpu.SemaphoreType.DMA((2,))],
        compiler_params=_params(("arbitrary",)),
        name="moe_dispatch",
    )(row_tok, xn)


def _moe_up_kernel(blk_ref, split_ref, e_ref, next_e_ref, next_split_ref, flag_ref,
                   x_ref, wg_hbm, wu_hbm, o_ref, sg_ref, su_ref, wgb_ref, wub_ref, sem):
    k = pl.program_id(0)
    flag = flag_ref[k]
    th = o_ref.shape[1]
    cw = min(th, 256)

    def start_tile(e, s):
        cols = pl.ds(pl.multiple_of(s * th, th), th)
        pltpu.make_async_copy(wg_hbm.at[e, :, cols], sg_ref, sem.at[0]).start()
        pltpu.make_async_copy(wu_hbm.at[e, :, cols], su_ref, sem.at[1]).start()

    def wait_tile():
        pltpu.make_async_copy(wg_hbm.at[0, :, pl.ds(0, th)], sg_ref, sem.at[0]).wait()
        pltpu.make_async_copy(wu_hbm.at[0, :, pl.ds(0, th)], su_ref, sem.at[1]).wait()

    def project(convert):
        for c in range(th // cw):
            cols = slice(c * cw, (c + 1) * cw)
            if convert:
                wgb_ref[:, cols] = sg_ref[:, cols].astype(BF16)
                wub_ref[:, cols] = su_ref[:, cols].astype(BF16)
            g = jnp.dot(x_ref[...], wgb_ref[:, cols], preferred_element_type=F32)
            u = jnp.dot(x_ref[...], wub_ref[:, cols], preferred_element_type=F32)
            o_ref[:, cols] = (g * _sigmoid(g) * u).astype(o_ref.dtype)

    @pl.when((flag & FIRST) != 0)
    def _():
        @pl.when(k == 0)
        def _():
            start_tile(e_ref[0], split_ref[0])

        wait_tile()
        project(True)

        @pl.when((flag & HAS_NEXT) != 0)
        def _():
            start_tile(next_e_ref[k], next_split_ref[k])

    @pl.when((flag & (USED | FIRST)) == USED)
    def _():
        project(False)

    @pl.when((flag & USED) == 0)
    def _():
        o_ref[...] = jnp.zeros_like(o_ref)


def _moe_up(xs, items, w_gate, w_up, blk, n_split):
    n_rows, D = xs.shape
    hid = w_gate.shape[2]
    th = hid // n_split
    n_items = items[0].shape[0]
    grid_spec = pltpu.PrefetchScalarGridSpec(
        num_scalar_prefetch=6,
        grid=(n_items,),
        in_specs=[pl.BlockSpec((blk, D), lambda k, b, s, e, ne, ns, f: (b[k], 0)),
                  pl.BlockSpec(memory_space=pl.ANY),
                  pl.BlockSpec(memory_space=pl.ANY)],
        out_specs=pl.BlockSpec((blk, th), lambda k, b, s, e, ne, ns, f: (b[k], s[k])),
        scratch_shapes=[pltpu.VMEM((D, th), F32), pltpu.VMEM((D, th), F32),
                        pltpu.VMEM((D, th), BF16), pltpu.VMEM((D, th), BF16),
                        pltpu.SemaphoreType.DMA((2,))],
    )
    return pl.pallas_call(
        _moe_up_kernel,
        grid_spec=grid_spec,
        out_shape=jax.ShapeDtypeStruct((n_rows, hid), BF16),
        compiler_params=_params(("arbitrary",)),
        name="moe_up",
    )(*items, xs, w_gate, w_up)


def _moe_down_kernel(e_ref, next_e_ref, flag_ref, h_ref, wd_hbm, o_ref, stage_ref, wdb_ref, sem):
    i = pl.program_id(0)
    flag = flag_ref[i]
    cw = min(o_ref.shape[1], 256)

    def copy(e):
        return pltpu.make_async_copy(wd_hbm.at[e], stage_ref, sem)

    def project(convert):
        for c in range(o_ref.shape[1] // cw):
            cols = slice(c * cw, (c + 1) * cw)
            if convert:
                wdb_ref[:, cols] = stage_ref[:, cols].astype(BF16)
            o_ref[:, cols] = jnp.dot(h_ref[...], wdb_ref[:, cols], preferred_element_type=F32)

    @pl.when((flag & FIRST) != 0)
    def _():
        @pl.when(i == 0)
        def _():
            copy(e_ref[0]).start()

        copy(e_ref[i]).wait()
        project(True)

        @pl.when((flag & HAS_NEXT) != 0)
        def _():
            copy(next_e_ref[i]).start()

    @pl.when((flag & (USED | FIRST)) == USED)
    def _():
        project(False)

    @pl.when((flag & USED) == 0)
    def _():
        o_ref[...] = jnp.zeros_like(o_ref)


def _moe_down(hid, block_tables, w_down, blk):
    n_rows, H = hid.shape
    D = w_down.shape[2]
    grid_spec = pltpu.PrefetchScalarGridSpec(
        num_scalar_prefetch=3,
        grid=(n_rows // blk,),
        in_specs=[pl.BlockSpec((blk, H), lambda i, e, ne, f: (i, 0)),
                  pl.BlockSpec(memory_space=pl.ANY)],
        out_specs=pl.BlockSpec((blk, D), lambda i, e, ne, f: (i, 0)),
        scratch_shapes=[pltpu.VMEM((H, D), F32), pltpu.VMEM((H, D), BF16),
                        pltpu.SemaphoreType.DMA(())],
    )
    return pl.pallas_call(
        _moe_down_kernel,
        grid_spec=grid_spec,
        out_shape=jax.ShapeDtypeStruct((n_rows, D), F32),
        compiler_params=_params(("arbitrary",)),
        name="moe_down",
    )(*block_tables, hid, w_down)


def _combine_kernel(pos_hbm, y_hbm, x_ref, w_ref, g_ref, o_ref,
                    idx_smem, ybuf, idx_sem, row_sem, *, tm, n_tiles, final_norm):
    i = pl.program_id(0)
    n_idx = TOP_K * tm

    def idx_copy(t):
        return pltpu.make_async_copy(pos_hbm.at[pl.ds(pl.multiple_of(t * n_idx, n_idx), n_idx)],
                                     idx_smem.at[t % 2], idx_sem)

    def row_copy(src, slot, dst):
        return pltpu.make_async_copy(y_hbm.at[pl.ds(src, 1), :], ybuf.at[slot, pl.ds(dst, 1), :],
                                     row_sem.at[slot])

    def issue_rows(t):
        slot = t % 2

        def body(r, c):
            for k in range(TOP_K):
                row_copy(idx_smem[slot, TOP_K * r + k], slot, k * tm + r).start()
            return c
        lax.fori_loop(0, tm, body, 0, unroll=8)

    def wait_rows(slot):
        pltpu.make_async_copy(y_hbm.at[pl.ds(0, n_idx), :], ybuf.at[slot], row_sem.at[slot]).wait()

    @pl.when(i == 0)
    def _():
        idx_copy(0).start()
        idx_copy(0).wait()
        issue_rows(0)
        if n_tiles > 1:
            idx_copy(1).start()

    @pl.when(i + 1 < n_tiles)
    def _():
        idx_copy(i + 1).wait()
        issue_rows(i + 1)

    @pl.when(i + 2 < n_tiles)
    def _():
        idx_copy(i + 2).start()

    slot = i % 2
    wait_rows(slot)
    w = w_ref[...]
    out = x_ref[...] + w[:, 0:1] * ybuf[slot, 0:tm, :] + w[:, 1:2] * ybuf[slot, tm:2 * tm, :]
    if final_norm:
        out = _rms(out, g_ref[...])
    o_ref[...] = out


def _combine(x1, y, pos, w_assign, g_final, final_norm):
    T, D = x1.shape
    tm = _tile(T, 256)
    n_tiles = T // tm
    kern = functools.partial(_combine_kernel, tm=tm, n_tiles=n_tiles, final_norm=final_norm)
    return pl.pallas_call(
        kern,
        grid=(n_tiles,),
        in_specs=[pl.BlockSpec(memory_space=pl.ANY),
                  pl.BlockSpec(memory_space=pl.ANY),
                  pl.BlockSpec((tm, D), lambda i: (i, 0)),
                  pl.BlockSpec((tm, TOP_K), lambda i: (i, 0)),
                  pl.BlockSpec((1, D), lambda i: (0, 0))],
        out_specs=pl.BlockSpec((tm, D), lambda i: (i, 0)),
        out_shape=jax.ShapeDtypeStruct((T, D), F32),
        scratch_shapes=[pltpu.SMEM((2, TOP_K * tm), jnp.int32),
                        pltpu.VMEM((2, TOP_K * tm, D), F32),
                        pltpu.SemaphoreType.DMA(()),
                        pltpu.SemaphoreType.DMA((2,))],
        compiler_params=_params(("arbitrary",)),
        name="moe_combine",
    )(pos.reshape(-1), y, x1, w_assign, g_final.reshape(1, D))


def _layer(x, B, S, p, g_final, final_norm):
    T, D = x.shape
    W = D // 2
    xn = _norm_bf16(x, p["g_mix"])
    proj = _in_proj(xn, p["w_in"], p["b_gate"], W)
    a = _sgu(proj, p["g_sgu"], p["w_sgu"], p["b_sgu"], W)
    b = _attention(proj, B, S, W)
    merged = _merge(a, b, proj, p["w_proj_a"], p["w_proj_b"], W, D)
    x1 = _out_proj(merged, p["w_out"], x)

    route_tm, (xn2, e_out, p_out, tile_counts) = _route(
        x1, p["g_ffn"], p["w_group"], p["b_group"], p["w_router"], p["b_router"])
    n_exp = p["w_gate"].shape[0]
    blk = 256
    n_split = 2
    n_blocks = T * TOP_K // blk + n_exp
    pos, row_tok, items, block_tables = _dispatch_tables(
        e_out[:, :TOP_K], e_out[:, TOP_K:2 * TOP_K], tile_counts[:, 0, :n_exp],
        route_tm, n_exp, blk, n_blocks, n_split)
    xs = _dispatch(xn2, row_tok, blk, n_blocks)
    hid = _moe_up(xs, items, p["w_gate"], p["w_up"], blk, n_split)
    y = _moe_down(hid, block_tables, p["w_down"], blk)
    return _combine(x1, y, pos, p_out[:, :TOP_K], g_final, final_norm)


def kernel(x, g_mix, w_in, g_sgu, w_sgu, b_sgu, b_gate, w_proj_a, w_proj_b, w_out,
           g_ffn, w_group, b_group, w_router, b_router, w_gate, w_up, w_down, g_final):
    B, S, D = x.shape
    stacked = dict(g_mix=g_mix, w_in=w_in, g_sgu=g_sgu, w_sgu=w_sgu, b_sgu=b_sgu, b_gate=b_gate,
                   w_proj_a=w_proj_a, w_proj_b=w_proj_b, w_out=w_out, g_ffn=g_ffn,
                   w_group=w_group, b_group=b_group, w_router=w_router, b_router=b_router,
                   w_gate=w_gate, w_up=w_up, w_down=w_down)
    depth = w_in.shape[0]
    xf = x.reshape(B * S, D)
    for l in range(depth):
        p = {name: val[l] for name, val in stacked.items()}
        xf = _layer(xf, B, S, p, g_final, final_norm=(l == depth - 1))
    return xf.reshape(B, S, D)
```

```python
import functools

import jax
import jax.numpy as jnp
from jax import lax
from jax.experimental import pallas as pl
from jax.experimental.pallas import tpu as pltpu

F32 = jnp.float32
BF16 = jnp.bfloat16

EPS = 1e-6
CHUNK = 64
SGU_BLOCK = 128
SGU_GROUPS = 8
HEAD_DIM = 128
N_GROUPS_MOE = 4
EXPERTS_PER_GROUP = 8
TOP_K = 2
ROUTE_LANES = 128
USED, FIRST, HAS_NEXT = 1, 2, 4

SKIP_BELOW = 120.0

VMEM_LIMIT = 56 * 2**20


def _tile(n, pref):
    t = min(n, pref)
    assert n % t == 0, (n, pref)
    return t


def _params(sem):
    return pltpu.CompilerParams(dimension_semantics=sem, vmem_limit_bytes=VMEM_LIMIT)


def _sigmoid(x):
    return 1.0 / (1.0 + jnp.exp(-x))


def _rms(x, g):
    return x * lax.rsqrt(jnp.mean(x * x, axis=-1, keepdims=True) + EPS) * g


def _norm_kernel(x_ref, g_ref, o_ref):
    o_ref[...] = _rms(x_ref[...], g_ref[...]).astype(o_ref.dtype)


def _norm_bf16(x, g):
    T, D = x.shape
    tm = _tile(T, 512)
    return pl.pallas_call(
        _norm_kernel,
        grid=(T // tm,),
        in_specs=[pl.BlockSpec((tm, D), lambda i: (i, 0)),
                  pl.BlockSpec((1, D), lambda i: (0, 0))],
        out_specs=pl.BlockSpec((tm, D), lambda i: (i, 0)),
        out_shape=jax.ShapeDtypeStruct((T, D), BF16),
        compiler_params=_params(("parallel",)),
        name="norm_bf16",
    )(x, g.reshape(1, D))


def _load_column_tile(w_hbm, stage_ref, wb_ref, sem, j, n_tiles):
    tn = stage_ref.shape[1]

    def copy(t):
        return pltpu.make_async_copy(w_hbm.at[:, pl.ds(pl.multiple_of(t * tn, tn), tn)], stage_ref, sem)

    @pl.when(j == 0)
    def _():
        copy(0).start()

    copy(j).wait()
    wb_ref[...] = stage_ref[...].astype(BF16)

    @pl.when(j + 1 < n_tiles)
    def _():
        copy(j + 1).start()


def _proj_kernel(x_ref, w_hbm, c_ref, o_ref, stage_ref, wb_ref, sem, *, n_chunks):
    @pl.when(pl.program_id(1) == 0)
    def _():
        _load_column_tile(w_hbm, stage_ref, wb_ref, sem, pl.program_id(0), pl.num_programs(0))

    cw = o_ref.shape[1] // n_chunks
    for c in range(n_chunks):
        cols = slice(c * cw, (c + 1) * cw)
        y = jnp.dot(x_ref[...], wb_ref[:, cols], preferred_element_type=F32)
        a0, a1 = c_ref[0:1, cols], c_ref[1:2, cols]
        b0, b1, b3 = c_ref[2:3, cols], c_ref[3:4, cols], c_ref[4:5, cols]
        o_ref[:, cols] = ((a0 + a1 * y) * _sigmoid(b0 + y * (b1 + b3 * (y * y)))).astype(o_ref.dtype)


def _proj_epilogue_coefficients(b_gate, W):
    c2 = 2.0 * 0.7978845608028654
    n_gate = b_gate.size
    seg = lambda v, n: jnp.full((n,), v, F32)
    a0 = jnp.concatenate([seg(0.0, 5 * W), seg(1.0, n_gate)])
    a1 = jnp.concatenate([seg(1.0, 2 * W), seg(HEAD_DIM ** -0.5, W), seg(1.0, 2 * W), seg(0.0, n_gate)])
    b0 = jnp.concatenate([seg(0.0, 2 * W), seg(1e30, 3 * W), b_gate.reshape(-1).astype(F32)])
    b1 = jnp.concatenate([seg(c2, 2 * W), seg(0.0, 3 * W), seg(1.0, n_gate)])
    b3 = jnp.concatenate([seg(c2 * 0.044715, 2 * W), seg(0.0, 3 * W + n_gate)])
    zero = jnp.zeros_like(a0)
    return jnp.stack([a0, a1, b0, b1, b3, zero, zero, zero])


def _in_proj(xn, w_in, b_gate, W):
    T, D = xn.shape
    N = w_in.shape[1]
    tm = _tile(T, 1024)
    tn = _tile(W, 1024)
    coef = _proj_epilogue_coefficients(b_gate, W)
    assert coef.shape == (8, N)
    kern = functools.partial(_proj_kernel, n_chunks=max(tn // 256, 1))
    return pl.pallas_call(
        kern,
        grid=(N // tn, T // tm),
        in_specs=[pl.BlockSpec((tm, D), lambda j, i: (i, 0)),
                  pl.BlockSpec(memory_space=pl.ANY),
                  pl.BlockSpec((8, tn), lambda j, i: (0, j))],
        out_specs=pl.BlockSpec((tm, tn), lambda j, i: (i, j)),
        out_shape=jax.ShapeDtypeStruct((T, N), BF16),
        scratch_shapes=[pltpu.VMEM((D, tn), F32), pltpu.VMEM((D, tn), BF16),
                        pltpu.SemaphoreType.DMA(())],
        compiler_params=_params(("arbitrary", "arbitrary")),
        name="in_proj",
    )(xn, w_in, coef)


def _sgu_kernel(u_ref, v_ref, g_ref, w_ref, bt_ref, o_ref, *, n_sub, gd):
    v = v_ref[...].astype(F32)
    vn = _rms(v, g_ref[...]).astype(BF16)
    t_chunk = lax.broadcasted_iota(jnp.int32, (SGU_BLOCK, SGU_BLOCK), 0) // CHUNK
    s_chunk = lax.broadcasted_iota(jnp.int32, (SGU_BLOCK, SGU_BLOCK), 1) // CHUNK
    causal = s_chunk <= t_chunk
    for g in range(SGU_GROUPS):
        w = jnp.where(causal, w_ref[g], 0.0).astype(BF16)
        bias = bt_ref[:, g:g + 1]
        cols = slice(g * gd, (g + 1) * gd)
        for n in range(n_sub):
            rows = slice(n * SGU_BLOCK, (n + 1) * SGU_BLOCK)
            mixed = jnp.dot(w, vn[rows, cols], preferred_element_type=F32) + bias
            o_ref[rows, cols] = (u_ref[rows, cols].astype(F32) * mixed).astype(o_ref.dtype)


def _sgu(proj, g_sgu, w_sgu, b_sgu, W):
    T = proj.shape[0]
    tm = _tile(T, 2 * SGU_BLOCK)
    gd = W // SGU_GROUPS
    kern = functools.partial(_sgu_kernel, n_sub=tm // SGU_BLOCK, gd=gd)
    return pl.pallas_call(
        kern,
        grid=(T // tm,),
        in_specs=[pl.BlockSpec((tm, W), lambda i: (i, 0)),
                  pl.BlockSpec((tm, W), lambda i: (i, 1)),
                  pl.BlockSpec((1, W), lambda i: (0, 0)),
                  pl.BlockSpec((SGU_GROUPS, SGU_BLOCK, SGU_BLOCK), lambda i: (0, 0, 0)),
                  pl.BlockSpec((SGU_BLOCK, SGU_GROUPS), lambda i: (0, 0))],
        out_specs=pl.BlockSpec((tm, W), lambda i: (i, 0)),
        out_shape=jax.ShapeDtypeStruct((T, W), BF16),
        compiler_params=_params(("parallel",)),
        name="sgu",
    )(proj, proj, g_sgu.reshape(1, W), w_sgu, b_sgu.T)


def _attn_kernel(q_ref, k_ref, v_ref, o_ref, acc_ref, carry_ref, *, tq, heads):
    qi = pl.program_id(2)
    jj = lax.broadcasted_iota(jnp.int32, (tq, tq), 0)
    ss = lax.broadcasted_iota(jnp.int32, (tq, tq), 1)
    tri = (jj >= ss).astype(BF16)
    before = ss < jj

    def step(kb, diagonal):
        rows = pl.ds(pl.multiple_of(kb * tq, tq), tq)
        hcols = [slice(h * HEAD_DIM, (h + 1) * HEAD_DIM) for h in range(heads)]
        zs = [lax.dot_general(q_ref[:, c], k_ref[rows, c], (((1,), (1,)), ((), ())),
                              preferred_element_type=F32) for c in hcols]
        splits = []
        for z in zs:
            lg = -(jnp.maximum(z, 0.0) + jnp.log(1.0 + jnp.exp(-jnp.abs(z))))
            if diagonal:
                lg = jnp.where(before, lg, 0.0)
            hi = lg.astype(BF16)
            splits.append((hi, (lg - hi.astype(F32)).astype(BF16)))
        incls = [jnp.dot(hi, tri, preferred_element_type=F32)
                 + jnp.dot(lo, tri, preferred_element_type=F32) for hi, lo in splits]
        weights, worst = [], None
        for h, (z, incl) in enumerate(zip(zs, incls)):
            if diagonal:
                a = jnp.where(before, jnp.exp(z + incl), 0.0)
                carry = incl[:, 0:1]
            else:
                a = jnp.exp(z + incl + carry_ref[h])
                carry = carry_ref[h] + incl[:, 0:1]
            carry_ref[h] = carry
            weights.append(a.astype(BF16))
            worst = carry if worst is None else jnp.maximum(worst, carry)
        for c, a in zip(hcols, weights):
            pv = jnp.dot(a, v_ref[rows, c], preferred_element_type=F32)
            if diagonal:
                acc_ref[:, c] = pv
            else:
                acc_ref[:, c] += pv
        return jnp.max(worst)

    def cond(state):
        kb, worst = state
        return (kb >= 0) & (worst > -SKIP_BELOW)

    def body(state):
        kb, _ = state
        return kb - 1, step(kb, False)

    lax.while_loop(cond, body, (qi - 1, step(qi, True)))
    o_ref[...] = acc_ref[...].astype(o_ref.dtype)


def _attention(proj, B, S, W):
    T = proj.shape[0]
    n_heads = W // HEAD_DIM
    heads = min(4, n_heads)
    assert n_heads % heads == 0
    hw = heads * HEAD_DIM
    tq = _tile(S, 256)
    nq = S // tq
    q0, k0, v0 = 2 * W // hw, 3 * W // hw, 4 * W // hw
    kern = functools.partial(_attn_kernel, tq=tq, heads=heads)
    return pl.pallas_call(
        kern,
        grid=(B, n_heads // heads, nq),
        in_specs=[pl.BlockSpec((tq, hw), lambda b, g, i: (b * nq + i, q0 + g)),
                  pl.BlockSpec((S, hw), lambda b, g, i: (b, k0 + g)),
                  pl.BlockSpec((S, hw), lambda b, g, i: (b, v0 + g))],
        out_specs=pl.BlockSpec((tq, hw), lambda b, g, i: (b * nq + i, g)),
        out_shape=jax.ShapeDtypeStruct((T, W), BF16),
        scratch_shapes=[pltpu.VMEM((tq, hw), F32),
                        pltpu.VMEM((heads, tq, 1), F32)],
        compiler_params=_params(("parallel", "parallel", "arbitrary")),
        name="stick_breaking",
    )(proj, proj, proj)


def _merge_kernel(a_ref, b_ref, pa_hbm, pb_hbm, ga_ref, gb_ref, o_ref,
                  sa_ref, sb_ref, pab_ref, pbb_ref, sem):
    @pl.when(pl.program_id(1) == 0)
    def _():
        j, n = pl.program_id(0), pl.num_programs(0)
        _load_column_tile(pa_hbm, sa_ref, pab_ref, sem.at[0], j, n)
        _load_column_tile(pb_hbm, sb_ref, pbb_ref, sem.at[1], j, n)

    cw = min(o_ref.shape[1], 256)
    for c in range(o_ref.shape[1] // cw):
        cols = slice(c * cw, (c + 1) * cw)
        ya = jnp.dot(a_ref[...], pab_ref[:, cols], preferred_element_type=F32)
        yb = jnp.dot(b_ref[...], pbb_ref[:, cols], preferred_element_type=F32)
        o_ref[:, cols] = (ga_ref[:, cols].astype(F32) * ya
                          + gb_ref[:, cols].astype(F32) * yb).astype(o_ref.dtype)


def _merge(a, b, proj, w_pa, w_pb, W, D):
    T = a.shape[0]
    tm = _tile(T, 512)
    tn = _tile(W, 1024)
    ga0 = 5 * W // tn
    gb0 = (5 * W + D) // tn
    return pl.pallas_call(
        _merge_kernel,
        grid=(D // tn, T // tm),
        in_specs=[pl.BlockSpec((tm, W), lambda j, i: (i, 0)),
                  pl.BlockSpec((tm, W), lambda j, i: (i, 0)),
                  pl.BlockSpec(memory_space=pl.ANY),
                  pl.BlockSpec(memory_space=pl.ANY),
                  pl.BlockSpec((tm, tn), lambda j, i: (i, ga0 + j)),
                  pl.BlockSpec((tm, tn), lambda j, i: (i, gb0 + j))],
        out_specs=pl.BlockSpec((tm, tn), lambda j, i: (i, j)),
        out_shape=jax.ShapeDtypeStruct((T, D), BF16),
        scratch_shapes=[pltpu.VMEM((W, tn), F32), pltpu.VMEM((W, tn), F32),
                        pltpu.VMEM((W, tn), BF16), pltpu.VMEM((W, tn), BF16),
                        pltpu.SemaphoreType.DMA((2,))],
        compiler_params=_params(("arbitrary", "arbitrary")),
        name="merge",
    )(a, b, w_pa, w_pb, proj, proj)


def _out_kernel(m_ref, w_hbm, x_ref, o_ref, stage_ref, wb_ref, sem):
    @pl.when(pl.program_id(1) == 0)
    def _():
        _load_column_tile(w_hbm, stage_ref, wb_ref, sem, pl.program_id(0), pl.num_programs(0))

    cw = min(o_ref.shape[1], 256)
    for c in range(o_ref.shape[1] // cw):
        cols = slice(c * cw, (c + 1) * cw)
        o_ref[:, cols] = x_ref[:, cols] + jnp.dot(m_ref[...], wb_ref[:, cols],
                                                  preferred_element_type=F32)


def _out_proj(merged, w_out, x):
    T, D = x.shape
    tm = _tile(T, 512)
    tn = _tile(D, 1024)
    return pl.pallas_call(
        _out_kernel,
        grid=(D // tn, T // tm),
        in_specs=[pl.BlockSpec((tm, D), lambda j, i: (i, 0)),
                  pl.BlockSpec(memory_space=pl.ANY),
                  pl.BlockSpec((tm, tn), lambda j, i: (i, j))],
        out_specs=pl.BlockSpec((tm, tn), lambda j, i: (i, j)),
        out_shape=jax.ShapeDtypeStruct((T, D), F32),
        scratch_shapes=[pltpu.VMEM((D, tn), F32), pltpu.VMEM((D, tn), BF16),
                        pltpu.SemaphoreType.DMA(())],
        compiler_params=_params(("arbitrary", "arbitrary")),
        name="out_proj",
    )(merged, w_out, x)


def _route_kernel(x_ref, g_ref, wr_ref, br_ref, xn_ref, e_ref, p_ref, c_ref):
    xn = _rms(x_ref[...], g_ref[...])
    xn_ref[...] = xn
    wr = wr_ref[...]
    wr_hi = wr.astype(BF16)
    wr_lo = (wr - wr_hi.astype(F32)).astype(BF16)
    xn_hi = xn.astype(BF16)
    xn_lo = (xn - xn_hi.astype(F32)).astype(BF16)
    logits = (jnp.dot(xn_hi, wr_hi, preferred_element_type=F32)
              + jnp.dot(xn_lo, wr_hi, preferred_element_type=F32)
              + jnp.dot(xn_hi, wr_lo, preferred_element_type=F32)) + br_ref[...]
    tm = logits.shape[0]
    lane = lax.broadcasted_iota(jnp.int32, logits.shape, 1)
    neg = -jnp.inf

    def top(vals):
        m = jnp.max(vals, axis=1, keepdims=True)
        idx = jnp.min(jnp.where(vals == m, lane, ROUTE_LANES), axis=1, keepdims=True)
        return m, idx

    is_group = lane < N_GROUPS_MOE
    gmax, grp = top(jnp.where(is_group, logits, neg))
    p_grp = 1.0 / jnp.sum(jnp.where(is_group, jnp.exp(logits - gmax), 0.0), axis=1, keepdims=True)
    first = N_GROUPS_MOE + grp * EXPERTS_PER_GROUP
    local = jnp.where((lane >= first) & (lane < first + EXPERTS_PER_GROUP), logits, neg)
    m1, i1 = top(local)
    m2, i2 = top(jnp.where(lane == i1, neg, local))
    e2 = jnp.exp(m2 - m1)
    w1 = p_grp / (1.0 + e2)
    w2 = p_grp * e2 / (1.0 + e2)
    ex1 = i1 - N_GROUPS_MOE
    ex2 = i2 - N_GROUPS_MOE

    hot1 = lane == ex1
    hot2 = lane == ex2
    chosen = (hot1 | hot2).astype(BF16)
    row = lax.broadcasted_iota(jnp.int32, (tm, tm), 0)
    col = lax.broadcasted_iota(jnp.int32, (tm, tm), 1)
    earlier = jnp.dot((col < row).astype(BF16), chosen, preferred_element_type=F32)
    r1 = jnp.sum(jnp.where(hot1, earlier, 0.0), axis=1, keepdims=True).astype(jnp.int32)
    r2 = jnp.sum(jnp.where(hot2, earlier, 0.0), axis=1, keepdims=True).astype(jnp.int32)
    counts = jnp.sum(chosen.astype(F32), axis=0, keepdims=True).astype(jnp.int32)

    e_ref[...] = jnp.where(lane == 0, ex1, jnp.where(lane == 1, ex2,
                           jnp.where(lane == 2, r1, jnp.where(lane == 3, r2, 0))))
    p_ref[...] = jnp.where(lane == 0, w1, jnp.where(lane == 1, w2, 0.0))
    c_ref[...] = jnp.broadcast_to(counts[None], c_ref.shape)


def _route(x1, g_ffn, w_group, b_group, w_router, b_router):
    T, D = x1.shape
    n_logit = w_group.shape[1] + w_router.shape[1]
    assert n_logit <= ROUTE_LANES
    wr = jnp.zeros((D, ROUTE_LANES), F32).at[:, :n_logit].set(jnp.concatenate([w_group, w_router], axis=1))
    br = jnp.zeros((1, ROUTE_LANES), F32).at[0, :n_logit].set(jnp.concatenate([b_group, b_router]))
    tm = _tile(T, 256)
    n_tiles = T // tm
    return tm, pl.pallas_call(
        _route_kernel,
        grid=(n_tiles,),
        in_specs=[pl.BlockSpec((tm, D), lambda i: (i, 0)),
                  pl.BlockSpec((1, D), lambda i: (0, 0)),
                  pl.BlockSpec((D, ROUTE_LANES), lambda i: (0, 0)),
                  pl.BlockSpec((1, ROUTE_LANES), lambda i: (0, 0))],
        out_specs=[pl.BlockSpec((tm, D), lambda i: (i, 0)),
                   pl.BlockSpec((tm, ROUTE_LANES), lambda i: (i, 0)),
                   pl.BlockSpec((tm, ROUTE_LANES), lambda i: (i, 0)),
                   pl.BlockSpec((1, 8, ROUTE_LANES), lambda i: (i, 0, 0))],
        out_shape=[jax.ShapeDtypeStruct((T, D), F32),
                   jax.ShapeDtypeStruct((T, ROUTE_LANES), jnp.int32),
                   jax.ShapeDtypeStruct((T, ROUTE_LANES), F32),
                   jax.ShapeDtypeStruct((n_tiles, 8, ROUTE_LANES), jnp.int32)],
        compiler_params=_params(("parallel",)),
        name="route",
    )(x1, g_ffn.reshape(1, D), wr, br)


def _dispatch_tables(experts, ranks, tile_counts, route_tm, n_exp, blk, n_blocks, n_split):
    T = experts.shape[0]
    i32 = jnp.int32
    tile_base = jnp.cumsum(tile_counts, axis=0) - tile_counts
    counts = jnp.sum(tile_counts, axis=0)
    padded = (counts + blk - 1) // blk * blk
    pad_end = jnp.cumsum(padded)
    pad_start = pad_end - padded
    base = jnp.repeat(tile_base + pad_start[None, :], route_tm, axis=0)
    hot = experts[:, :, None] == jnp.arange(n_exp, dtype=i32)[None, None, :]
    pos = (jnp.sum(jnp.where(hot, base[:, None, :], 0), axis=2) + ranks).astype(i32)
    token = jnp.broadcast_to(jnp.arange(T, dtype=i32)[:, None], pos.shape)
    pad_tok = jnp.arange(n_blocks * blk, dtype=i32) % T
    row_tok = pad_tok.at[pos.reshape(-1)].set(token.reshape(-1), unique_indices=True)

    n_used = pad_end[-1] // blk
    bstart = jnp.arange(n_blocks, dtype=i32) * blk
    be = jnp.minimum(jnp.sum((pad_end[None, :] <= bstart[:, None]).astype(i32), axis=1), n_exp - 1)
    be = jnp.where(jnp.arange(n_blocks) < n_used, be, be[n_used - 1])

    k = jnp.arange(n_blocks * n_split, dtype=i32)
    e_k = be[k // n_split]
    first_blk = pad_start[e_k] // blk
    n_blk = jnp.maximum(padded[e_k] // blk, 1)
    p = k - n_split * first_blk
    used = k < n_split * n_used
    idle = k - n_split * n_used
    item_blk = jnp.where(used, first_blk + p % n_blk, n_used + idle // n_split).astype(i32)
    item_split = jnp.where(used, p // n_blk, idle % n_split).astype(i32)
    item_first = used & (p % n_blk == 0)
    nxt = k + n_blk
    item_has_next = item_first & (nxt < n_split * n_used)
    nxt = jnp.minimum(nxt, n_blocks * n_split - 1)
    flags = (USED * used + FIRST * item_first + HAS_NEXT * item_has_next).astype(i32)
    items = (item_blk, item_split, e_k.astype(i32), e_k[nxt].astype(i32), item_split[nxt], flags)

    blk_id = jnp.arange(n_blocks, dtype=i32)
    blk_used = blk_id < n_used
    blk_first = blk_used & (blk_id == pad_start[be] // blk)
    nxt_blk = blk_id + jnp.maximum(padded[be] // blk, 1)
    blk_has_next = blk_first & (nxt_blk < n_used)
    nxt_blk = jnp.minimum(nxt_blk, n_blocks - 1)
    block_flags = (USED * blk_used + FIRST * blk_first + HAS_NEXT * blk_has_next).astype(i32)
    return pos, row_tok, items, (be.astype(i32), be[nxt_blk].astype(i32), block_flags)


def _dispatch_kernel(tok_hbm, x_hbm, o_ref, idx_smem, xbuf, idx_sem, row_sem, *, blk, n_blocks):
    i = pl.program_id(0)

    def idx_copy(b):
        return pltpu.make_async_copy(tok_hbm.at[pl.ds(pl.multiple_of(b * blk, blk), blk)],
                                     idx_smem.at[b % 2], idx_sem)

    def row_copy(tok, slot, r):
        return pltpu.make_async_copy(x_hbm.at[pl.ds(tok, 1), :], xbuf.at[slot, pl.ds(r, 1), :],
                                     row_sem.at[slot])

    def issue_rows(b):
        slot = b % 2

        def body(r, c):
            row_copy(idx_smem[slot, r], slot, r).start()
            return c
        lax.fori_loop(0, blk, body, 0, unroll=16)

    def wait_rows(slot):
        pltpu.make_async_copy(x_hbm.at[pl.ds(0, blk), :], xbuf.at[slot], row_sem.at[slot]).wait()

    @pl.when(i == 0)
    def _():
        idx_copy(0).start()
        idx_copy(0).wait()
        issue_rows(0)
        if n_blocks > 1:
            idx_copy(1).start()

    @pl.when(i + 1 < n_blocks)
    def _():
        idx_copy(i + 1).wait()
        issue_rows(i + 1)

    @pl.when(i + 2 < n_blocks)
    def _():
        idx_copy(i + 2).start()

    slot = i % 2
    wait_rows(slot)
    o_ref[...] = xbuf[slot].astype(o_ref.dtype)


def _dispatch(xn, row_tok, blk, n_blocks):
    T, D = xn.shape
    kern = functools.partial(_dispatch_kernel, blk=blk, n_blocks=n_blocks)
    return pl.pallas_call(
        kern,
        grid=(n_blocks,),
        in_specs=[pl.BlockSpec(memory_space=pl.ANY),
                  pl.BlockSpec(memory_space=pl.ANY)],
        out_specs=pl.BlockSpec((blk, D), lambda i: (i, 0)),
        out_shape=jax.ShapeDtypeStruct((n_blocks * blk, D), BF16),
        scratch_shapes=[pltpu.SMEM((2, blk), jnp.int32),
                        pltpu.VMEM((2, blk, D), F32),
                        pltpu.SemaphoreType.DMA(()),
                        pltpu.SemaphoreType.DMA((2,))],
        compiler_params=_params(("arbitrary",)),
        name="moe_dispatch",
    )(row_tok, xn)


def _moe_up_kernel(blk_ref, split_ref, e_ref, next_e_ref, next_split_ref, flag_ref,
                   x_ref, wg_hbm, wu_hbm, o_ref, sg_ref, su_ref, wgb_ref, wub_ref, sem):
    k = pl.program_id(0)
    flag = flag_ref[k]
    th = o_ref.shape[1]
    cw = min(th, 256)

    def start_tile(e, s):
        cols = pl.ds(pl.multiple_of(s * th, th), th)
        pltpu.make_async_copy(wg_hbm.at[e, :, cols], sg_ref, sem.at[0]).start()
        pltpu.make_async_copy(wu_hbm.at[e, :, cols], su_ref, sem.at[1]).start()

    def wait_tile():
        pltpu.make_async_copy(wg_hbm.at[0, :, pl.ds(0, th)], sg_ref, sem.at[0]).wait()
        pltpu.make_async_copy(wu_hbm.at[0, :, pl.ds(0, th)], su_ref, sem.at[1]).wait()

    def project(convert):
        for c in range(th // cw):
            cols = slice(c * cw, (c + 1) * cw)
            if convert:
                wgb_ref[:, cols] = sg_ref[:, cols].astype(BF16)
                wub_ref[:, cols] = su_ref[:, cols].astype(BF16)
            g = jnp.dot(x_ref[...], wgb_ref[:, cols], preferred_element_type=F32)
            u = jnp.dot(x_ref[...], wub_ref[:, cols], preferred_element_type=F32)
            o_ref[:, cols] = (g * _sigmoid(g) * u).astype(o_ref.dtype)

    @pl.when((flag & FIRST) != 0)
    def _():
        @pl.when(k == 0)
        def _():
            start_tile(e_ref[0], split_ref[0])

        wait_tile()
        project(True)

        @pl.when((flag & HAS_NEXT) != 0)
        def _():
            start_tile(next_e_ref[k], next_split_ref[k])

    @pl.when((flag & (USED | FIRST)) == USED)
    def _():
        project(False)

    @pl.when((flag & USED) == 0)
    def _():
        o_ref[...] = jnp.zeros_like(o_ref)


def _moe_up(xs, items, w_gate, w_up, blk, n_split):
    n_rows, D = xs.shape
    hid = w_gate.shape[2]
    th = hid // n_split
    n_items = items[0].shape[0]
    grid_spec = pltpu.PrefetchScalarGridSpec(
        num_scalar_prefetch=6,
        grid=(n_items,),
        in_specs=[pl.BlockSpec((blk, D), lambda k, b, s, e, ne, ns, f: (b[k], 0)),
                  pl.BlockSpec(memory_space=pl.ANY),
                  pl.BlockSpec(memory_space=pl.ANY)],
        out_specs=pl.BlockSpec((blk, th), lambda k, b, s, e, ne, ns, f: (b[k], s[k])),
        scratch_shapes=[pltpu.VMEM((D, th), F32), pltpu.VMEM((D, th), F32),
                        pltpu.VMEM((D, th), BF16), pltpu.VMEM((D, th), BF16),
                        pltpu.SemaphoreType.DMA((2,))],
    )
    return pl.pallas_call(
        _moe_up_kernel,
        grid_spec=grid_spec,
        out_shape=jax.ShapeDtypeStruct((n_rows, hid), BF16),
        compiler_params=_params(("arbitrary",)),
        name="moe_up",
    )(*items, xs, w_gate, w_up)


def _moe_down_kernel(e_ref, next_e_ref, flag_ref, h_ref, wd_hbm, o_ref, stage_ref, wdb_ref, sem):
    i = pl.program_id(0)
    flag = flag_ref[i]
    cw = min(o_ref.shape[1], 256)

    def copy(e):
        return pltpu.make_async_copy(wd_hbm.at[e], stage_ref, sem)

    def project(convert):
        for c in range(o_ref.shape[1] // cw):
            cols = slice(c * cw, (c + 1) * cw)
            if convert:
                wdb_ref[:, cols] = stage_ref[:, cols].astype(BF16)
            o_ref[:, cols] = jnp.dot(h_ref[...], wdb_ref[:, cols], preferred_element_type=F32)

    @pl.when((flag & FIRST) != 0)
    def _():
        @pl.when(i == 0)
        def _():
            copy(e_ref[0]).start()

        copy(e_ref[i]).wait()
        project(True)

        @pl.when((flag & HAS_NEXT) != 0)
        def _():
            copy(next_e_ref[i]).start()

    @pl.when((flag & (USED | FIRST)) == USED)
    def _():
        project(False)

    @pl.when((flag & USED) == 0)
    def _():
        o_ref[...] = jnp.zeros_like(o_ref)


def _moe_down(hid, block_tables, w_down, blk):
    n_rows, H = hid.shape
    D = w_down.shape[2]
    grid_spec = pltpu.PrefetchScalarGridSpec(
        num_scalar_prefetch=3,
        grid=(n_rows // blk,),
        in_specs=[pl.BlockSpec((blk, H), lambda i, e, ne, f: (i, 0)),
                  pl.BlockSpec(memory_space=pl.ANY)],
        out_specs=pl.BlockSpec((blk, D), lambda i, e, ne, f: (i, 0)),
        scratch_shapes=[pltpu.VMEM((H, D), F32), pltpu.VMEM((H, D), BF16),
                        pltpu.SemaphoreType.DMA(())],
    )
    return pl.pallas_call(
        _moe_down_kernel,
        grid_spec=grid_spec,
        out_shape=jax.ShapeDtypeStruct((n_rows, D), F32),
        compiler_params=_params(("arbitrary",)),
        name="moe_down",
    )(*block_tables, hid, w_down)


def _combine_kernel(pos_hbm, y_hbm, x_ref, w_ref, g_ref, o_ref,
                    idx_smem, ybuf, idx_sem, row_sem, *, tm, n_tiles, final_norm):
    i = pl.program_id(0)
    n_idx = TOP_K * tm

    def idx_copy(t):
        return pltpu.make_async_copy(pos_hbm.at[pl.ds(pl.multiple_of(t * n_idx, n_idx), n_idx)],
                                     idx_smem.at[t % 2], idx_sem)

    def row_copy(src, slot, dst):
        return pltpu.make_async_copy(y_hbm.at[pl.ds(src, 1), :], ybuf.at[slot, pl.ds(dst, 1), :],
                                     row_sem.at[slot])

    def issue_rows(t):
        slot = t % 2

        def body(r, c):
            for k in range(TOP_K):
                row_copy(idx_smem[slot, TOP_K * r + k], slot, k * tm + r).start()
            return c
        lax.fori_loop(0, tm, body, 0, unroll=8)

    def wait_rows(slot):
        pltpu.make_async_copy(y_hbm.at[pl.ds(0, n_idx), :], ybuf.at[slot], row_sem.at[slot]).wait()

    @pl.when(i == 0)
    def _():
        idx_copy(0).start()
        idx_copy(0).wait()
        issue_rows(0)
        if n_tiles > 1:
            idx_copy(1).start()

    @pl.when(i + 1 < n_tiles)
    def _():
        idx_copy(i + 1).wait()
        issue_rows(i + 1)

    @pl.when(i + 2 < n_tiles)
    def _():
        idx_copy(i + 2).start()

    slot = i % 2
    wait_rows(slot)
    w = w_ref[...]
    out = x_ref[...] + w[:, 0:1] * ybuf[slot, 0:tm, :] + w[:, 1:2] * ybuf[slot, tm:2 * tm, :]
    if final_norm:
        out = _rms(out, g_ref[...])
    o_ref[...] = out


def _combine(x1, y, pos, w_assign, g_final, final_norm):
    T, D = x1.shape
    tm = _tile(T, 256)
    n_tiles = T // tm
    kern = functools.partial(_combine_kernel, tm=tm, n_tiles=n_tiles, final_norm=final_norm)
    return pl.pallas_call(
        kern,
        grid=(n_tiles,),
        in_specs=[pl.BlockSpec(memory_space=pl.ANY),
                  pl.BlockSpec(memory_space=pl.ANY),
                  pl.BlockSpec((tm, D), lambda i: (i, 0)),
                  pl.BlockSpec((tm, TOP_K), lambda i: (i, 0)),
                  pl.BlockSpec((1, D), lambda i: (0, 0))],
        out_specs=pl.BlockSpec((tm, D), lambda i: (i, 0)),
        out_shape=jax.ShapeDtypeStruct((T, D), F32),
        scratch_shapes=[pltpu.SMEM((2, TOP_K * tm), jnp.int32),
                        pltpu.VMEM((2, TOP_K * tm, D), F32),
                        pltpu.SemaphoreType.DMA(()),
                        pltpu.SemaphoreType.DMA((2,))],
        compiler_params=_params(("arbitrary",)),
        name="moe_combine",
    )(pos.reshape(-1), y, x1, w_assign, g_final.reshape(1, D))


def _layer(x, B, S, p, g_final, final_norm):
    T, D = x.shape
    W = D // 2
    xn = _norm_bf16(x, p["g_mix"])
    proj = _in_proj(xn, p["w_in"], p["b_gate"], W)
    a = _sgu(proj, p["g_sgu"], p["w_sgu"], p["b_sgu"], W)
    b = _attention(proj, B, S, W)
    merged = _merge(a, b, proj, p["w_proj_a"], p["w_proj_b"], W, D)
    x1 = _out_proj(merged, p["w_out"], x)

    route_tm, (xn2, e_out, p_out, tile_counts) = _route(
        x1, p["g_ffn"], p["w_group"], p["b_group"], p["w_router"], p["b_router"])
    n_exp = p["w_gate"].shape[0]
    blk = 256
    n_split = 2
    n_blocks = T * TOP_K // blk + n_exp
    pos, row_tok, items, block_tables = _dispatch_tables(
        e_out[:, :TOP_K], e_out[:, TOP_K:2 * TOP_K], tile_counts[:, 0, :n_exp],
        route_tm, n_exp, blk, n_blocks, n_split)
    xs = _dispatch(xn2, row_tok, blk, n_blocks)
    hid = _moe_up(xs, items, p["w_gate"], p["w_up"], blk, n_split)
    y = _moe_down(hid, block_tables, p["w_down"], blk)
    return _combine(x1, y, pos, p_out[:, :TOP_K], g_final, final_norm)


def kernel(x, g_mix, w_in, g_sgu, w_sgu, b_sgu, b_gate, w_proj_a, w_proj_b, w_out,
           g_ffn, w_group, b_group, w_router, b_router, w_gate, w_up, w_down, g_final):
    B, S, D = x.shape
    stacked = dict(g_mix=g_mix, w_in=w_in, g_sgu=g_sgu, w_sgu=w_sgu, b_sgu=b_sgu, b_gate=b_gate,
                   w_proj_a=w_proj_a, w_proj_b=w_proj_b, w_out=w_out, g_ffn=g_ffn,
                   w_group=w_group, b_group=b_group, w_router=w_router, b_router=b_router,
                   w_gate=w_gate, w_up=w_up, w_down=w_down)
    depth = w_in.shape[0]
    xf = x.reshape(B * S, D)
    for l in range(depth):
        p = {name: val[l] for name, val in stacked.items()}
        xf = _layer(xf, B, S, p, g_final, final_norm=(l == depth - 1))
    return xf.reshape(B, S, D)
```

```python
import functools

import jax
import jax.numpy as jnp
from jax import lax
from jax.experimental import pallas as pl
from jax.experimental.pallas import tpu as pltpu

F32 = jnp.float32
BF16 = jnp.bfloat16

EPS = 1e-6
CHUNK = 64
SGU_BLOCK = 128
SGU_GROUPS = 8
HEAD_DIM = 128
N_GROUPS_MOE = 4
EXPERTS_PER_GROUP = 8
TOP_K = 2
ROUTE_LANES = 128
USED, FIRST, HAS_NEXT = 1, 2, 4

SKIP_BELOW = 120.0

VMEM_LIMIT = 56 * 2**20


def _tile(n, pref):
    t = min(n, pref)
    assert n % t == 0, (n, pref)
    return t


def _params(sem):
    return pltpu.CompilerParams(dimension_semantics=sem, vmem_limit_bytes=VMEM_LIMIT)


def _sigmoid(x):
    return 0.5 * (1.0 + jnp.tanh(0.5 * x))


def _rms(x, g):
    return x * lax.rsqrt(jnp.mean(x * x, axis=-1, keepdims=True) + EPS) * g


def _norm_kernel(x_ref, g_ref, o_ref):
    o_ref[...] = _rms(x_ref[...], g_ref[...]).astype(o_ref.dtype)


def _norm_bf16(x, g):
    T, D = x.shape
    tm = _tile(T, 512)
    return pl.pallas_call(
        _norm_kernel,
        grid=(T // tm,),
        in_specs=[pl.BlockSpec((tm, D), lambda i: (i, 0)),
                  pl.BlockSpec((1, D), lambda i: (0, 0))],
        out_specs=pl.BlockSpec((tm, D), lambda i: (i, 0)),
        out_shape=jax.ShapeDtypeStruct((T, D), BF16),
        compiler_params=_params(("parallel",)),
        name="norm_bf16",
    )(x, g.reshape(1, D))


def _load_column_tile(w_hbm, stage_ref, wb_ref, sem, j, n_tiles):
    tn = stage_ref.shape[1]

    def copy(t):
        return pltpu.make_async_copy(w_hbm.at[:, pl.ds(pl.multiple_of(t * tn, tn), tn)], stage_ref, sem)

    @pl.when(j == 0)
    def _():
        copy(0).start()

    copy(j).wait()
    wb_ref[...] = stage_ref[...].astype(BF16)

    @pl.when(j + 1 < n_tiles)
    def _():
        copy(j + 1).start()


def _proj_kernel(x_ref, w_hbm, c_ref, o_ref, stage_ref, wb_ref, sem, *, n_chunks):
    @pl.when(pl.program_id(1) == 0)
    def _():
        _load_column_tile(w_hbm, stage_ref, wb_ref, sem, pl.program_id(0), pl.num_programs(0))

    cw = o_ref.shape[1] // n_chunks
    for c in range(n_chunks):
        cols = slice(c * cw, (c + 1) * cw)
        y = jnp.dot(x_ref[...], wb_ref[:, cols], preferred_element_type=F32)
        a0, a1 = c_ref[0:1, cols], c_ref[1:2, cols]
        b0, b1, b3 = c_ref[2:3, cols], c_ref[3:4, cols], c_ref[4:5, cols]
        o_ref[:, cols] = ((a0 + a1 * y) * (1.0 + jnp.tanh(b0 + y * (b1 + b3 * (y * y))))).astype(o_ref.dtype)


def _proj_epilogue_coefficients(b_gate, W):
    c1 = 0.7978845608028654
    n_gate = b_gate.size
    seg = lambda v, n: jnp.full((n,), v, F32)
    a0 = jnp.concatenate([seg(0.0, 5 * W), seg(0.5, n_gate)])
    a1 = jnp.concatenate([seg(0.5, 2 * W), seg(0.5 * HEAD_DIM ** -0.5, W), seg(0.5, 2 * W), seg(0.0, n_gate)])
    b0 = jnp.concatenate([seg(0.0, 2 * W), seg(1e30, 3 * W), 0.5 * b_gate.reshape(-1).astype(F32)])
    b1 = jnp.concatenate([seg(c1, 2 * W), seg(0.0, 3 * W), seg(0.5, n_gate)])
    b3 = jnp.concatenate([seg(c1 * 0.044715, 2 * W), seg(0.0, 3 * W + n_gate)])
    zero = jnp.zeros_like(a0)
    return jnp.stack([a0, a1, b0, b1, b3, zero, zero, zero])


def _in_proj(xn, w_in, b_gate, W):
    T, D = xn.shape
    N = w_in.shape[1]
    tm = _tile(T, 1024)
    tn = _tile(W, 1024)
    coef = _proj_epilogue_coefficients(b_gate, W)
    assert coef.shape == (8, N)
    kern = functools.partial(_proj_kernel, n_chunks=max(tn // 256, 1))
    return pl.pallas_call(
        kern,
        grid=(N // tn, T // tm),
        in_specs=[pl.BlockSpec((tm, D), lambda j, i: (i, 0)),
                  pl.BlockSpec(memory_space=pl.ANY),
                  pl.BlockSpec((8, tn), lambda j, i: (0, j))],
        out_specs=pl.BlockSpec((tm, tn), lambda j, i: (i, j)),
        out_shape=jax.ShapeDtypeStruct((T, N), BF16),
        scratch_shapes=[pltpu.VMEM((D, tn), F32), pltpu.VMEM((D, tn), BF16),
                        pltpu.SemaphoreType.DMA(())],
        compiler_params=_params(("arbitrary", "arbitrary")),
        name="in_proj",
    )(xn, w_in, coef)


def _sgu_kernel(u_ref, v_ref, g_ref, w_ref, bt_ref, o_ref, *, n_sub, gd):
    v = v_ref[...].astype(F32)
    vn = _rms(v, g_ref[...]).astype(BF16)
    t_chunk = lax.broadcasted_iota(jnp.int32, (SGU_BLOCK, SGU_BLOCK), 0) // CHUNK
    s_chunk = lax.broadcasted_iota(jnp.int32, (SGU_BLOCK, SGU_BLOCK), 1) // CHUNK
    causal = s_chunk <= t_chunk
    for g in range(SGU_GROUPS):
        w = jnp.where(causal, w_ref[g], 0.0).astype(BF16)
        bias = bt_ref[:, g:g + 1]
        cols = slice(g * gd, (g + 1) * gd)
        for n in range(n_sub):
            rows = slice(n * SGU_BLOCK, (n + 1) * SGU_BLOCK)
            mixed = jnp.dot(w, vn[rows, cols], preferred_element_type=F32) + bias
            o_ref[rows, cols] = (u_ref[rows, cols].astype(F32) * mixed).astype(o_ref.dtype)


def _sgu(proj, g_sgu, w_sgu, b_sgu, W):
    T = proj.shape[0]
    tm = _tile(T, 2 * SGU_BLOCK)
    gd = W // SGU_GROUPS
    kern = functools.partial(_sgu_kernel, n_sub=tm // SGU_BLOCK, gd=gd)
    return pl.pallas_call(
        kern,
        grid=(T // tm,),
        in_specs=[pl.BlockSpec((tm, W), lambda i: (i, 0)),
                  pl.BlockSpec((tm, W), lambda i: (i, 1)),
                  pl.BlockSpec((1, W), lambda i: (0, 0)),
                  pl.BlockSpec((SGU_GROUPS, SGU_BLOCK, SGU_BLOCK), lambda i: (0, 0, 0)),
                  pl.BlockSpec((SGU_BLOCK, SGU_GROUPS), lambda i: (0, 0))],
        out_specs=pl.BlockSpec((tm, W), lambda i: (i, 0)),
        out_shape=jax.ShapeDtypeStruct((T, W), BF16),
        compiler_params=_params(("parallel",)),
        name="sgu",
    )(proj, proj, g_sgu.reshape(1, W), w_sgu, b_sgu.T)


def _attn_kernel(q_ref, k_ref, v_ref, o_ref, acc_ref, carry_ref, *, tq, heads):
    qi = pl.program_id(2)
    jj = lax.broadcasted_iota(jnp.int32, (tq, tq), 0)
    ss = lax.broadcasted_iota(jnp.int32, (tq, tq), 1)
    tri = (jj >= ss).astype(BF16)
    before = ss < jj

    def step(kb, diagonal):
        rows = pl.ds(pl.multiple_of(kb * tq, tq), tq)
        hcols = [slice(h * HEAD_DIM, (h + 1) * HEAD_DIM) for h in range(heads)]
        zs = [lax.dot_general(q_ref[:, c], k_ref[rows, c], (((1,), (1,)), ((), ())),
                              preferred_element_type=F32) for c in hcols]
        logs = []
        for z in zs:
            lg = -(jnp.maximum(z, 0.0) + jnp.log(1.0 + jnp.exp(-jnp.abs(z))))
            if diagonal:
                lg = jnp.where(before, lg, 0.0)
            logs.append(lg.astype(BF16))
        incls = [jnp.dot(lg, tri, preferred_element_type=F32) for lg in logs]
        weights, worst = [], None
        for h, (z, incl) in enumerate(zip(zs, incls)):
            if diagonal:
                a = jnp.where(before, jnp.exp(z + incl), 0.0)
                carry = incl[:, 0:1]
            else:
                a = jnp.exp(z + incl + carry_ref[h])
                carry = carry_ref[h] + incl[:, 0:1]
            carry_ref[h] = carry
            weights.append(a.astype(BF16))
            worst = carry if worst is None else jnp.maximum(worst, carry)
        for c, a in zip(hcols, weights):
            pv = jnp.dot(a, v_ref[rows, c], preferred_element_type=F32)
            if diagonal:
                acc_ref[:, c] = pv
            else:
                acc_ref[:, c] += pv
        return jnp.max(worst)

    def cond(state):
        kb, worst = state
        return (kb >= 0) & (worst > -SKIP_BELOW)

    def body(state):
        kb, _ = state
        return kb - 1, step(kb, False)

    lax.while_loop(cond, body, (qi - 1, step(qi, True)))
    o_ref[...] = acc_ref[...].astype(o_ref.dtype)


def _attention(proj, B, S, W):
    T = proj.shape[0]
    n_heads = W // HEAD_DIM
    heads = min(4, n_heads)
    assert n_heads % heads == 0
    hw = heads * HEAD_DIM
    tq = _tile(S, 256)
    nq = S // tq
    q0, k0, v0 = 2 * W // hw, 3 * W // hw, 4 * W // hw
    kern = functools.partial(_attn_kernel, tq=tq, heads=heads)
    return pl.pallas_call(
        kern,
        grid=(B, n_heads // heads, nq),
        in_specs=[pl.BlockSpec((tq, hw), lambda b, g, i: (b * nq + i, q0 + g)),
                  pl.BlockSpec((S, hw), lambda b, g, i: (b, k0 + g)),
                  pl.BlockSpec((S, hw), lambda b, g, i: (b, v0 + g))],
        out_specs=pl.BlockSpec((tq, hw), lambda b, g, i: (b * nq + i, g)),
        out_shape=jax.ShapeDtypeStruct((T, W), BF16),
        scratch_shapes=[pltpu.VMEM((tq, hw), F32),
                        pltpu.VMEM((heads, tq, 1), F32)],
        compiler_params=_params(("parallel", "parallel", "arbitrary")),
        name="stick_breaking",
    )(proj, proj, proj)


def _merge_kernel(a_ref, b_ref, pa_hbm, pb_hbm, ga_ref, gb_ref, o_ref,
                  sa_ref, sb_ref, pab_ref, pbb_ref, sem):
    @pl.when(pl.program_id(1) == 0)
    def _():
        j, n = pl.program_id(0), pl.num_programs(0)
        _load_column_tile(pa_hbm, sa_ref, pab_ref, sem.at[0], j, n)
        _load_column_tile(pb_hbm, sb_ref, pbb_ref, sem.at[1], j, n)

    cw = min(o_ref.shape[1], 256)
    for c in range(o_ref.shape[1] // cw):
        cols = slice(c * cw, (c + 1) * cw)
        ya = jnp.dot(a_ref[...], pab_ref[:, cols], preferred_element_type=F32)
        yb = jnp.dot(b_ref[...], pbb_ref[:, cols], preferred_element_type=F32)
        o_ref[:, cols] = (ga_ref[:, cols].astype(F32) * ya
                          + gb_ref[:, cols].astype(F32) * yb).astype(o_ref.dtype)


def _merge(a, b, proj, w_pa, w_pb, W, D):
    T = a.shape[0]
    tm = _tile(T, 512)
    tn = _tile(W, 1024)
    ga0 = 5 * W // tn
    gb0 = (5 * W + D) // tn
    return pl.pallas_call(
        _merge_kernel,
        grid=(D // tn, T // tm),
        in_specs=[pl.BlockSpec((tm, W), lambda j, i: (i, 0)),
                  pl.BlockSpec((tm, W), lambda j, i: (i, 0)),
                  pl.BlockSpec(memory_space=pl.ANY),
                  pl.BlockSpec(memory_space=pl.ANY),
                  pl.BlockSpec((tm, tn), lambda j, i: (i, ga0 + j)),
                  pl.BlockSpec((tm, tn), lambda j, i: (i, gb0 + j))],
        out_specs=pl.BlockSpec((tm, tn), lambda j, i: (i, j)),
        out_shape=jax.ShapeDtypeStruct((T, D), BF16),
        scratch_shapes=[pltpu.VMEM((W, tn), F32), pltpu.VMEM((W, tn), F32),
                        pltpu.VMEM((W, tn), BF16), pltpu.VMEM((W, tn), BF16),
                        pltpu.SemaphoreType.DMA((2,))],
        compiler_params=_params(("arbitrary", "arbitrary")),
        name="merge",
    )(a, b, w_pa, w_pb, proj, proj)


def _out_kernel(m_ref, w_hbm, x_ref, o_ref, stage_ref, wb_ref, sem):
    @pl.when(pl.program_id(1) == 0)
    def _():
        _load_column_tile(w_hbm, stage_ref, wb_ref, sem, pl.program_id(0), pl.num_programs(0))

    cw = min(o_ref.shape[1], 256)
    for c in range(o_ref.shape[1] // cw):
        cols = slice(c * cw, (c + 1) * cw)
        o_ref[:, cols] = x_ref[:, cols] + jnp.dot(m_ref[...], wb_ref[:, cols],
                                                  preferred_element_type=F32)


def _out_proj(merged, w_out, x):
    T, D = x.shape
    tm = _tile(T, 512)
    tn = _tile(D, 1024)
    return pl.pallas_call(
        _out_kernel,
        grid=(D // tn, T // tm),
        in_specs=[pl.BlockSpec((tm, D), lambda j, i: (i, 0)),
                  pl.BlockSpec(memory_space=pl.ANY),
                  pl.BlockSpec((tm, tn), lambda j, i: (i, j))],
        out_specs=pl.BlockSpec((tm, tn), lambda j, i: (i, j)),
        out_shape=jax.ShapeDtypeStruct((T, D), F32),
        scratch_shapes=[pltpu.VMEM((D, tn), F32), pltpu.VMEM((D, tn), BF16),
                        pltpu.SemaphoreType.DMA(())],
        compiler_params=_params(("arbitrary", "arbitrary")),
        name="out_proj",
    )(merged, w_out, x)


def _route_kernel(x_ref, g_ref, wr_ref, br_ref, xn_ref, e_ref, p_ref, c_ref):
    xn = _rms(x_ref[...], g_ref[...])
    xn_ref[...] = xn
    wr = wr_ref[...]
    wr_hi = wr.astype(BF16)
    wr_lo = (wr - wr_hi.astype(F32)).astype(BF16)
    xn_hi = xn.astype(BF16)
    xn_lo = (xn - xn_hi.astype(F32)).astype(BF16)
    logits = (jnp.dot(xn_hi, wr_hi, preferred_element_type=F32)
              + jnp.dot(xn_lo, wr_hi, preferred_element_type=F32)
              + jnp.dot(xn_hi, wr_lo, preferred_element_type=F32)) + br_ref[...]
    tm = logits.shape[0]
    lane = lax.broadcasted_iota(jnp.int32, logits.shape, 1)
    neg = -jnp.inf

    def top(vals):
        m = jnp.max(vals, axis=1, keepdims=True)
        idx = jnp.min(jnp.where(vals == m, lane, ROUTE_LANES), axis=1, keepdims=True)
        return m, idx

    is_group = lane < N_GROUPS_MOE
    gmax, grp = top(jnp.where(is_group, logits, neg))
    p_grp = 1.0 / jnp.sum(jnp.where(is_group, jnp.exp(logits - gmax), 0.0), axis=1, keepdims=True)
    first = N_GROUPS_MOE + grp * EXPERTS_PER_GROUP
    local = jnp.where((lane >= first) & (lane < first + EXPERTS_PER_GROUP), logits, neg)
    m1, i1 = top(local)
    m2, i2 = top(jnp.where(lane == i1, neg, local))
    e2 = jnp.exp(m2 - m1)
    w1 = p_grp / (1.0 + e2)
    w2 = p_grp * e2 / (1.0 + e2)
    ex1 = i1 - N_GROUPS_MOE
    ex2 = i2 - N_GROUPS_MOE

    hot1 = lane == ex1
    hot2 = lane == ex2
    chosen = (hot1 | hot2).astype(BF16)
    row = lax.broadcasted_iota(jnp.int32, (tm, tm), 0)
    col = lax.broadcasted_iota(jnp.int32, (tm, tm), 1)
    earlier = jnp.dot((col < row).astype(BF16), chosen, preferred_element_type=F32)
    r1 = jnp.sum(jnp.where(hot1, earlier, 0.0), axis=1, keepdims=True).astype(jnp.int32)
    r2 = jnp.sum(jnp.where(hot2, earlier, 0.0), axis=1, keepdims=True).astype(jnp.int32)
    counts = jnp.sum(chosen.astype(F32), axis=0, keepdims=True).astype(jnp.int32)

    e_ref[...] = jnp.where(lane == 0, ex1, jnp.where(lane == 1, ex2,
                           jnp.where(lane == 2, r1, jnp.where(lane == 3, r2, 0))))
    p_ref[...] = jnp.where(lane == 0, w1, jnp.where(lane == 1, w2, 0.0))
    c_ref[...] = jnp.broadcast_to(counts[None], c_ref.shape)


def _route(x1, g_ffn, w_group, b_group, w_router, b_router):
    T, D = x1.shape
    n_logit = w_group.shape[1] + w_router.shape[1]
    assert n_logit <= ROUTE_LANES
    wr = jnp.zeros((D, ROUTE_LANES), F32).at[:, :n_logit].set(jnp.concatenate([w_group, w_router], axis=1))
    br = jnp.zeros((1, ROUTE_LANES), F32).at[0, :n_logit].set(jnp.concatenate([b_group, b_router]))
    tm = _tile(T, 256)
    n_tiles = T // tm
    return tm, pl.pallas_call(
        _route_kernel,
        grid=(n_tiles,),
        in_specs=[pl.BlockSpec((tm, D), lambda i: (i, 0)),
                  pl.BlockSpec((1, D), lambda i: (0, 0)),
                  pl.BlockSpec((D, ROUTE_LANES), lambda i: (0, 0)),
                  pl.BlockSpec((1, ROUTE_LANES), lambda i: (0, 0))],
        out_specs=[pl.BlockSpec((tm, D), lambda i: (i, 0)),
                   pl.BlockSpec((tm, ROUTE_LANES), lambda i: (i, 0)),
                   pl.BlockSpec((tm, ROUTE_LANES), lambda i: (i, 0)),
                   pl.BlockSpec((1, 8, ROUTE_LANES), lambda i: (i, 0, 0))],
        out_shape=[jax.ShapeDtypeStruct((T, D), F32),
                   jax.ShapeDtypeStruct((T, ROUTE_LANES), jnp.int32),
                   jax.ShapeDtypeStruct((T, ROUTE_LANES), F32),
                   jax.ShapeDtypeStruct((n_tiles, 8, ROUTE_LANES), jnp.int32)],
        compiler_params=_params(("parallel",)),
        name="route",
    )(x1, g_ffn.reshape(1, D), wr, br)


def _dispatch_tables(experts, ranks, tile_counts, route_tm, n_exp, blk, n_blocks, n_split):
    T = experts.shape[0]
    i32 = jnp.int32
    tile_base = jnp.cumsum(tile_counts, axis=0) - tile_counts
    counts = jnp.sum(tile_counts, axis=0)
    padded = (counts + blk - 1) // blk * blk
    pad_end = jnp.cumsum(padded)
    pad_start = pad_end - padded
    base = jnp.repeat(tile_base + pad_start[None, :], route_tm, axis=0)
    hot = experts[:, :, None] == jnp.arange(n_exp, dtype=i32)[None, None, :]
    pos = (jnp.sum(jnp.where(hot, base[:, None, :], 0), axis=2) + ranks).astype(i32)
    token = jnp.broadcast_to(jnp.arange(T, dtype=i32)[:, None], pos.shape)
    pad_tok = jnp.arange(n_blocks * blk, dtype=i32) % T
    row_tok = pad_tok.at[pos.reshape(-1)].set(token.reshape(-1), unique_indices=True)

    n_used = pad_end[-1] // blk
    bstart = jnp.arange(n_blocks, dtype=i32) * blk
    be = jnp.minimum(jnp.sum((pad_end[None, :] <= bstart[:, None]).astype(i32), axis=1), n_exp - 1)
    be = jnp.where(jnp.arange(n_blocks) < n_used, be, be[n_used - 1])

    k = jnp.arange(n_blocks * n_split, dtype=i32)
    e_k = be[k // n_split]
    first_blk = pad_start[e_k] // blk
    n_blk = jnp.maximum(padded[e_k] // blk, 1)
    p = k - n_split * first_blk
    used = k < n_split * n_used
    idle = k - n_split * n_used
    item_blk = jnp.where(used, first_blk + p % n_blk, n_used + idle // n_split).astype(i32)
    item_split = jnp.where(used, p // n_blk, idle % n_split).astype(i32)
    item_first = used & (p % n_blk == 0)
    nxt = k + n_blk
    item_has_next = item_first & (nxt < n_split * n_used)
    nxt = jnp.minimum(nxt, n_blocks * n_split - 1)
    flags = (USED * used + FIRST * item_first + HAS_NEXT * item_has_next).astype(i32)
    items = (item_blk, item_split, e_k.astype(i32), e_k[nxt].astype(i32), item_split[nxt], flags)

    blk_id = jnp.arange(n_blocks, dtype=i32)
    blk_used = blk_id < n_used
    blk_first = blk_used & (blk_id == pad_start[be] // blk)
    nxt_blk = blk_id + jnp.maximum(padded[be] // blk, 1)
    blk_has_next = blk_first & (nxt_blk < n_used)
    nxt_blk = jnp.minimum(nxt_blk, n_blocks - 1)
    block_flags = (USED * blk_used + FIRST * blk_first + HAS_NEXT * blk_has_next).astype(i32)
    return pos, row_tok, items, (be.astype(i32), be[nxt_blk].astype(i32), block_flags)


def _dispatch_kernel(tok_hbm, x_hbm, o_ref, idx_smem, xbuf, idx_sem, row_sem, *, blk, n_blocks):
    i = pl.program_id(0)

    def idx_copy(b):
        return pltpu.make_async_copy(tok_hbm.at[pl.ds(pl.multiple_of(b * blk, blk), blk)],
                                     idx_smem.at[b % 2], idx_sem)

    def row_copy(tok, slot, r):
        return pltpu.make_async_copy(x_hbm.at[pl.ds(tok, 1), :], xbuf.at[slot, pl.ds(r, 1), :],
                                     row_sem.at[slot])

    def issue_rows(b):
        slot = b % 2

        def body(r, c):
            row_copy(idx_smem[slot, r], slot, r).start()
            return c
        lax.fori_loop(0, blk, body, 0, unroll=16)

    def wait_rows(slot):
        pltpu.make_async_copy(x_hbm.at[pl.ds(0, blk), :], xbuf.at[slot], row_sem.at[slot]).wait()

    @pl.when(i == 0)
    def _():
        idx_copy(0).start()
        idx_copy(0).wait()
        issue_rows(0)
        if n_blocks > 1:
            idx_copy(1).start()

    @pl.when(i + 1 < n_blocks)
    def _():
        idx_copy(i + 1).wait()
        issue_rows(i + 1)

    @pl.when(i + 2 < n_blocks)
    def _():
        idx_copy(i + 2).start()

    slot = i % 2
    wait_rows(slot)
    o_ref[...] = xbuf[slot].astype(o_ref.dtype)


def _dispatch(xn, row_tok, blk, n_blocks):
    T, D = xn.shape
    kern = functools.partial(_dispatch_kernel, blk=blk, n_blocks=n_blocks)
    return pl.pallas_call(
        kern,
        grid=(n_blocks,),
        in_specs=[pl.BlockSpec(memory_space=pl.ANY),
                  pl.BlockSpec(memory_space=pl.ANY)],
        out_specs=pl.BlockSpec((blk, D), lambda i: (i, 0)),
        out_shape=jax.ShapeDtypeStruct((n_blocks * blk, D), BF16),
        scratch_shapes=[pltpu.SMEM((2, blk), jnp.int32),
                        pltpu.VMEM((2, blk, D), F32),
                        pltpu.SemaphoreType.DMA(()),
                        pltpu.SemaphoreType.DMA((2,))],
        compiler_params=_params(("arbitrary",)),
        name="moe_dispatch",
    )(row_tok, xn)


def _moe_up_kernel(blk_ref, split_ref, e_ref, next_e_ref, next_split_ref, flag_ref,
                   x_ref, wg_hbm, wu_hbm, o_ref, sg_ref, su_ref, wgb_ref, wub_ref, sem):
    k = pl.program_id(0)
    flag = flag_ref[k]
    th = o_ref.shape[1]
    cw = min(th, 256)

    def start_tile(e, s):
        cols = pl.ds(pl.multiple_of(s * th, th), th)
        pltpu.make_async_copy(wg_hbm.at[e, :, cols], sg_ref, sem.at[0]).start()
        pltpu.make_async_copy(wu_hbm.at[e, :, cols], su_ref, sem.at[1]).start()

    def wait_tile():
        pltpu.make_async_copy(wg_hbm.at[0, :, pl.ds(0, th)], sg_ref, sem.at[0]).wait()
        pltpu.make_async_copy(wu_hbm.at[0, :, pl.ds(0, th)], su_ref, sem.at[1]).wait()

    def project(convert):
        for c in range(th // cw):
            cols = slice(c * cw, (c + 1) * cw)
            if convert:
                wgb_ref[:, cols] = sg_ref[:, cols].astype(BF16)
                wub_ref[:, cols] = su_ref[:, cols].astype(BF16)
            g = jnp.dot(x_ref[...], wgb_ref[:, cols], preferred_element_type=F32)
            u = jnp.dot(x_ref[...], wub_ref[:, cols], preferred_element_type=F32)
            o_ref[:, cols] = (g * _sigmoid(g) * u).astype(o_ref.dtype)

    @pl.when((flag & FIRST) != 0)
    def _():
        @pl.when(k == 0)
        def _():
            start_tile(e_ref[0], split_ref[0])

        wait_tile()
        project(True)

        @pl.when((flag & HAS_NEXT) != 0)
        def _():
            start_tile(next_e_ref[k], next_split_ref[k])

    @pl.when((flag & (USED | FIRST)) == USED)
    def _():
        project(False)

    @pl.when((flag & USED) == 0)
    def _():
        o_ref[...] = jnp.zeros_like(o_ref)


def _moe_up(xs, items, w_gate, w_up, blk, n_split):
    n_rows, D = xs.shape
    hid = w_gate.shape[2]
    th = hid // n_split
    n_items = items[0].shape[0]
    grid_spec = pltpu.PrefetchScalarGridSpec(
        num_scalar_prefetch=6,
        grid=(n_items,),
        in_specs=[pl.BlockSpec((blk, D), lambda k, b, s, e, ne, ns, f: (b[k], 0)),
                  pl.BlockSpec(memory_space=pl.ANY),
                  pl.BlockSpec(memory_space=pl.ANY)],
        out_specs=pl.BlockSpec((blk, th), lambda k, b, s, e, ne, ns, f: (b[k], s[k])),
        scratch_shapes=[pltpu.VMEM((D, th), F32), pltpu.VMEM((D, th), F32),
                        pltpu.VMEM((D, th), BF16), pltpu.VMEM((D, th), BF16),
                        pltpu.SemaphoreType.DMA((2,))],
    )
    return pl.pallas_call(
        _moe_up_kernel,
        grid_spec=grid_spec,
        out_shape=jax.ShapeDtypeStruct((n_rows, hid), BF16),
        compiler_params=_params(("arbitrary",)),
        name="moe_up",
    )(*items, xs, w_gate, w_up)


def _moe_down_kernel(e_ref, next_e_ref, flag_ref, h_ref, wd_hbm, o_ref, stage_ref, wdb_ref, sem):
    i = pl.program_id(0)
    flag = flag_ref[i]
    cw = min(o_ref.shape[1], 256)

    def copy(e):
        return pltpu.make_async_copy(wd_hbm.at[e], stage_ref, sem)

    def project(convert):
        for c in range(o_ref.shape[1] // cw):
            cols = slice(c * cw, (c + 1) * cw)
            if convert:
                wdb_ref[:, cols] = stage_ref[:, cols].astype(BF16)
            o_ref[:, cols] = jnp.dot(h_ref[...], wdb_ref[:, cols], preferred_element_type=F32)

    @pl.when((flag & FIRST) != 0)
    def _():
        @pl.when(i == 0)
        def _():
            copy(e_ref[0]).start()

        copy(e_ref[i]).wait()
        project(True)

        @pl.when((flag & HAS_NEXT) != 0)
        def _():
            copy(next_e_ref[i]).start()

    @pl.when((flag & (USED | FIRST)) == USED)
    def _():
        project(False)

    @pl.when((flag & USED) == 0)
    def _():
        o_ref[...] = jnp.zeros_like(o_ref)


def _moe_down(hid, block_tables, w_down, blk):
    n_rows, H = hid.shape
    D = w_down.shape[2]
    grid_spec = pltpu.PrefetchScalarGridSpec(
        num_scalar_prefetch=3,
        grid=(n_rows // blk,),
        in_specs=[pl.BlockSpec((blk, H), lambda i, e, ne, f: (i, 0)),
                  pl.BlockSpec(memory_space=pl.ANY)],
        out_specs=pl.BlockSpec((blk, D), lambda i, e, ne, f: (i, 0)),
        scratch_shapes=[pltpu.VMEM((H, D), F32), pltpu.VMEM((H, D), BF16),
                        pltpu.SemaphoreType.DMA(())],
    )
    return pl.pallas_call(
        _moe_down_kernel,
        grid_spec=grid_spec,
        out_shape=jax.ShapeDtypeStruct((n_rows, D), F32),
        compiler_params=_params(("arbitrary",)),
        name="moe_down",
    )(*block_tables, hid, w_down)


def _combine_kernel(pos_hbm, y_hbm, x_ref, w_ref, g_ref, o_ref,
                    idx_smem, ybuf, idx_sem, row_sem, *, tm, n_tiles, final_norm):
    i = pl.program_id(0)
    n_idx = TOP_K * tm

    def idx_copy(t):
        return pltpu.make_async_copy(pos_hbm.at[pl.ds(pl.multiple_of(t * n_idx, n_idx), n_idx)],
                                     idx_smem.at[t % 2], idx_sem)

    def row_copy(src, slot, dst):
        return pltpu.make_async_copy(y_hbm.at[pl.ds(src, 1), :], ybuf.at[slot, pl.ds(dst, 1), :],
                                     row_sem.at[slot])

    def issue_rows(t):
        slot = t % 2

        def body(r, c):
            for k in range(TOP_K):
                row_copy(idx_smem[slot, TOP_K * r + k], slot, k * tm + r).start()
            return c
        lax.fori_loop(0, tm, body, 0, unroll=8)

    def wait_rows(slot):
        pltpu.make_async_copy(y_hbm.at[pl.ds(0, n_idx), :], ybuf.at[slot], row_sem.at[slot]).wait()

    @pl.when(i == 0)
    def _():
        idx_copy(0).start()
        idx_copy(0).wait()
        issue_rows(0)
        if n_tiles > 1:
            idx_copy(1).start()

    @pl.when(i + 1 < n_tiles)
    def _():
        idx_copy(i + 1).wait()
        issue_rows(i + 1)

    @pl.when(i + 2 < n_tiles)
    def _():
        idx_copy(i + 2).start()

    slot = i % 2
    wait_rows(slot)
    w = w_ref[...]
    out = x_ref[...] + w[:, 0:1] * ybuf[slot, 0:tm, :] + w[:, 1:2] * ybuf[slot, tm:2 * tm, :]
    if final_norm:
        out = _rms(out, g_ref[...])
    o_ref[...] = out


def _combine(x1, y, pos, w_assign, g_final, final_norm):
    T, D = x1.shape
    tm = _tile(T, 256)
    n_tiles = T // tm
    kern = functools.partial(_combine_kernel, tm=tm, n_tiles=n_tiles, final_norm=final_norm)
    return pl.pallas_call(
        kern,
        grid=(n_tiles,),
        in_specs=[pl.BlockSpec(memory_space=pl.ANY),
                  pl.BlockSpec(memory_space=pl.ANY),
                  pl.BlockSpec((tm, D), lambda i: (i, 0)),
                  pl.BlockSpec((tm, TOP_K), lambda i: (i, 0)),
                  pl.BlockSpec((1, D), lambda i: (0, 0))],
        out_specs=pl.BlockSpec((tm, D), lambda i: (i, 0)),
        out_shape=jax.ShapeDtypeStruct((T, D), F32),
        scratch_shapes=[pltpu.SMEM((2, TOP_K * tm), jnp.int32),
                        pltpu.VMEM((2, TOP_K * tm, D), F32),
                        pltpu.SemaphoreType.DMA(()),
                        pltpu.SemaphoreType.DMA((2,))],
        compiler_params=_params(("arbitrary",)),
        name="moe_combine",
    )(pos.reshape(-1), y, x1, w_assign, g_final.reshape(1, D))


def _layer(x, B, S, p, g_final, final_norm):
    T, D = x.shape
    W = D // 2
    xn = _norm_bf16(x, p["g_mix"])
    proj = _in_proj(xn, p["w_in"], p["b_gate"], W)
    a = _sgu(proj, p["g_sgu"], p["w_sgu"], p["b_sgu"], W)
    b = _attention(proj, B, S, W)
    merged = _merge(a, b, proj, p["w_proj_a"], p["w_proj_b"], W, D)
    x1 = _out_proj(merged, p["w_out"], x)

    route_tm, (xn2, e_out, p_out, tile_counts) = _route(
        x1, p["g_ffn"], p["w_group"], p["b_group"], p["w_router"], p["b_router"])
    n_exp = p["w_gate"].shape[0]
    blk = 256
    n_split = 2
    n_blocks = T * TOP_K // blk + n_exp
    pos, row_tok, items, block_tables = _dispatch_tables(
        e_out[:, :TOP_K], e_out[:, TOP_K:2 * TOP_K], tile_counts[:, 0, :n_exp],
        route_tm, n_exp, blk, n_blocks, n_split)
    xs = _dispatch(xn2, row_tok, blk, n_blocks)
    hid = _moe_up(xs, items, p["w_gate"], p["w_up"], blk, n_split)
    y = _moe_down(hid, block_tables, p["w_down"], blk)
    return _combine(x1, y, pos, p_out[:, :TOP_K], g_final, final_norm)


def kernel(x, g_mix, w_in, g_sgu, w_sgu, b_sgu, b_gate, w_proj_a, w_proj_b, w_out,
           g_ffn, w_group, b_group, w_router, b_router, w_gate, w_up, w_down, g_final):
    B, S, D = x.shape
    stacked = dict(g_mix=g_mix, w_in=w_in, g_sgu=g_sgu, w_sgu=w_sgu, b_sgu=b_sgu, b_gate=b_gate,
                   w_proj_a=w_proj_a, w_proj_b=w_proj_b, w_out=w_out, g_ffn=g_ffn,
                   w_group=w_group, b_group=b_group, w_router=w_router, b_router=b_router,
                   w_gate=w_gate, w_up=w_up, w_down=w_down)
    depth = w_in.shape[0]
    xf = x.reshape(B * S, D)
    for l in range(depth):
        p = {name: val[l] for name, val in stacked.items()}
        xf = _layer(xf, B, S, p, g_final, final_norm=(l == depth - 1))
    return xf.reshape(B, S, D)
```

```python
import functools

import jax
import jax.numpy as jnp
from jax import lax
from jax.experimental import pallas as pl
from jax.experimental.pallas import tpu as pltpu

F32 = jnp.float32
BF16 = jnp.bfloat16

EPS = 1e-6
CHUNK = 64
SGU_BLOCK = 128
SGU_GROUPS = 8
HEAD_DIM = 128
N_GROUPS_MOE = 4
EXPERTS_PER_GROUP = 8
TOP_K = 2
ROUTE_LANES = 128
USED, FIRST, HAS_NEXT = 1, 2, 4

SKIP_BELOW = 120.0

VMEM_LIMIT = 56 * 2**20


def _tile(n, pref):
    t = min(n, pref)
    assert n % t == 0, (n, pref)
    return t


def _params(sem):
    return pltpu.CompilerParams(dimension_semantics=sem, vmem_limit_bytes=VMEM_LIMIT)


def _sigmoid(x):
    return 0.5 * (1.0 + jnp.tanh(0.5 * x))


def _rms(x, g):
    return x * lax.rsqrt(jnp.mean(x * x, axis=-1, keepdims=True) + EPS) * g


def _norm_kernel(x_ref, g_ref, o_ref):
    o_ref[...] = _rms(x_ref[...], g_ref[...]).astype(o_ref.dtype)


def _norm_bf16(x, g):
    T, D = x.shape
    tm = _tile(T, 512)
    return pl.pallas_call(
        _norm_kernel,
        grid=(T // tm,),
        in_specs=[pl.BlockSpec((tm, D), lambda i: (i, 0)),
                  pl.BlockSpec((1, D), lambda i: (0, 0))],
        out_specs=pl.BlockSpec((tm, D), lambda i: (i, 0)),
        out_shape=jax.ShapeDtypeStruct((T, D), BF16),
        compiler_params=_params(("parallel",)),
        name="norm_bf16",
    )(x, g.reshape(1, D))


def _load_column_tile(w_hbm, stage_ref, wb_ref, sem, j, n_tiles):
    tn = stage_ref.shape[1]

    def copy(t):
        return pltpu.make_async_copy(w_hbm.at[:, pl.ds(pl.multiple_of(t * tn, tn), tn)], stage_ref, sem)

    @pl.when(j == 0)
    def _():
        copy(0).start()

    copy(j).wait()
    wb_ref[...] = stage_ref[...].astype(BF16)

    @pl.when(j + 1 < n_tiles)
    def _():
        copy(j + 1).start()


def _proj_kernel(x_ref, w_hbm, c_ref, o_ref, stage_ref, wb_ref, sem, *, n_chunks):
    @pl.when(pl.program_id(1) == 0)
    def _():
        _load_column_tile(w_hbm, stage_ref, wb_ref, sem, pl.program_id(0), pl.num_programs(0))

    cw = o_ref.shape[1] // n_chunks
    for c in range(n_chunks):
        cols = slice(c * cw, (c + 1) * cw)
        y = jnp.dot(x_ref[...], wb_ref[:, cols], preferred_element_type=F32)
        a0, a1 = c_ref[0:1, cols], c_ref[1:2, cols]
        b0, b1, b3 = c_ref[2:3, cols], c_ref[3:4, cols], c_ref[4:5, cols]
        o_ref[:, cols] = ((a0 + a1 * y) * (1.0 + jnp.tanh(b0 + y * (b1 + b3 * (y * y))))).astype(o_ref.dtype)


def _proj_epilogue_coefficients(b_gate, W):
    c1 = 0.7978845608028654
    n_gate = b_gate.size
    seg = lambda v, n: jnp.full((n,), v, F32)
    a0 = jnp.concatenate([seg(0.0, 5 * W), seg(0.5, n_gate)])
    a1 = jnp.concatenate([seg(0.5, 2 * W), seg(0.5 * HEAD_DIM ** -0.5, W), seg(0.5, 2 * W), seg(0.0, n_gate)])
    b0 = jnp.concatenate([seg(0.0, 2 * W), seg(1e30, 3 * W), 0.5 * b_gate.reshape(-1).astype(F32)])
    b1 = jnp.concatenate([seg(c1, 2 * W), seg(0.0, 3 * W), seg(0.5, n_gate)])
    b3 = jnp.concatenate([seg(c1 * 0.044715, 2 * W), seg(0.0, 3 * W + n_gate)])
    zero = jnp.zeros_like(a0)
    return jnp.stack([a0, a1, b0, b1, b3, zero, zero, zero])


def _in_proj(xn, w_in, b_gate, W):
    T, D = xn.shape
    N = w_in.shape[1]
    tm = _tile(T, 1024)
    tn = _tile(W, 1024)
    coef = _proj_epilogue_coefficients(b_gate, W)
    assert coef.shape == (8, N)
    kern = functools.partial(_proj_kernel, n_chunks=max(tn // 256, 1))
    return pl.pallas_call(
        kern,
        grid=(N // tn, T // tm),
        in_specs=[pl.BlockSpec((tm, D), lambda j, i: (i, 0)),
                  pl.BlockSpec(memory_space=pl.ANY),
                  pl.BlockSpec((8, tn), lambda j, i: (0, j))],
        out_specs=pl.BlockSpec((tm, tn), lambda j, i: (i, j)),
        out_shape=jax.ShapeDtypeStruct((T, N), BF16),
        scratch_shapes=[pltpu.VMEM((D, tn), F32), pltpu.VMEM((D, tn), BF16),
                        pltpu.SemaphoreType.DMA(())],
        compiler_params=_params(("arbitrary", "arbitrary")),
        name="in_proj",
    )(xn, w_in, coef)


def _sgu_kernel(u_ref, v_ref, g_ref, w_ref, bt_ref, o_ref, *, n_sub, gd):
    v = v_ref[...].astype(F32)
    vn = _rms(v, g_ref[...]).astype(BF16)
    t_chunk = lax.broadcasted_iota(jnp.int32, (SGU_BLOCK, SGU_BLOCK), 0) // CHUNK
    s_chunk = lax.broadcasted_iota(jnp.int32, (SGU_BLOCK, SGU_BLOCK), 1) // CHUNK
    causal = s_chunk <= t_chunk
    for g in range(SGU_GROUPS):
        w = jnp.where(causal, w_ref[g], 0.0).astype(BF16)
        bias = bt_ref[:, g:g + 1]
        cols = slice(g * gd, (g + 1) * gd)
        for n in range(n_sub):
            rows = slice(n * SGU_BLOCK, (n + 1) * SGU_BLOCK)
            mixed = jnp.dot(w, vn[rows, cols], preferred_element_type=F32) + bias
            o_ref[rows, cols] = (u_ref[rows, cols].astype(F32) * mixed).astype(o_ref.dtype)


def _sgu(proj, g_sgu, w_sgu, b_sgu, W):
    T = proj.shape[0]
    tm = _tile(T, 2 * SGU_BLOCK)
    gd = W // SGU_GROUPS
    kern = functools.partial(_sgu_kernel, n_sub=tm // SGU_BLOCK, gd=gd)
    return pl.pallas_call(
        kern,
        grid=(T // tm,),
        in_specs=[pl.BlockSpec((tm, W), lambda i: (i, 0)),
                  pl.BlockSpec((tm, W), lambda i: (i, 1)),
                  pl.BlockSpec((1, W), lambda i: (0, 0)),
                  pl.BlockSpec((SGU_GROUPS, SGU_BLOCK, SGU_BLOCK), lambda i: (0, 0, 0)),
                  pl.BlockSpec((SGU_BLOCK, SGU_GROUPS), lambda i: (0, 0))],
        out_specs=pl.BlockSpec((tm, W), lambda i: (i, 0)),
        out_shape=jax.ShapeDtypeStruct((T, W), BF16),
        compiler_params=_params(("parallel",)),
        name="sgu",
    )(proj, proj, g_sgu.reshape(1, W), w_sgu, b_sgu.T)


def _attn_kernel(q_ref, k_ref, v_ref, o_ref, acc_ref, carry_ref, *, tq, heads):
    qi = pl.program_id(2)
    jj = lax.broadcasted_iota(jnp.int32, (tq, tq), 0)
    ss = lax.broadcasted_iota(jnp.int32, (tq, tq), 1)
    tri = (jj >= ss).astype(BF16)
    before = ss < jj

    def step(kb, diagonal):
        rows = pl.ds(pl.multiple_of(kb * tq, tq), tq)
        hcols = [slice(h * HEAD_DIM, (h + 1) * HEAD_DIM) for h in range(heads)]
        zs = [lax.dot_general(q_ref[:, c], k_ref[rows, c], (((1,), (1,)), ((), ())),
                              preferred_element_type=F32) for c in hcols]
        logs = []
        for z in zs:
            lg = -(jnp.maximum(z, 0.0) + jnp.log(1.0 + jnp.exp(-jnp.abs(z))))
            if diagonal:
                lg = jnp.where(before, lg, 0.0)
            logs.append(lg.astype(BF16))
        incls = [jnp.dot(lg, tri, preferred_element_type=F32) for lg in logs]
        weights, worst = [], None
        for h, (z, incl) in enumerate(zip(zs, incls)):
            if diagonal:
                a = jnp.where(before, jnp.exp(z + incl), 0.0)
                carry = incl[:, 0:1]
            else:
                a = jnp.exp(z + incl + carry_ref[h])
                carry = carry_ref[h] + incl[:, 0:1]
            carry_ref[h] = carry
            weights.append(a.astype(BF16))
            worst = carry if worst is None else jnp.maximum(worst, carry)
        for c, a in zip(hcols, weights):
            pv = jnp.dot(a, v_ref[rows, c], preferred_element_type=F32)
            if diagonal:
                acc_ref[:, c] = pv
            else:
                acc_ref[:, c] += pv
        return jnp.max(worst)

    def cond(state):
        kb, worst = state
        return (kb >= 0) & (worst > -SKIP_BELOW)

    def body(state):
        kb, _ = state
        return kb - 1, step(kb, False)

    lax.while_loop(cond, body, (qi - 1, step(qi, True)))
    o_ref[...] = acc_ref[...].astype(o_ref.dtype)


def _attention(proj, B, S, W):
    T = proj.shape[0]
    n_heads = W // HEAD_DIM
    heads = min(4, n_heads)
    assert n_heads % heads == 0
    hw = heads * HEAD_DIM
    tq = _tile(S, 256)
    nq = S // tq
    q0, k0, v0 = 2 * W // hw, 3 * W // hw, 4 * W // hw
    kern = functools.partial(_attn_kernel, tq=tq, heads=heads)
    return pl.pallas_call(
        kern,
        grid=(B, n_heads // heads, nq),
        in_specs=[pl.BlockSpec((tq, hw), lambda b, g, i: (b * nq + i, q0 + g)),
                  pl.BlockSpec((S, hw), lambda b, g, i: (b, k0 + g)),
                  pl.BlockSpec((S, hw), lambda b, g, i: (b, v0 + g))],
        out_specs=pl.BlockSpec((tq, hw), lambda b, g, i: (b * nq + i, g)),
        out_shape=jax.ShapeDtypeStruct((T, W), BF16),
        scratch_shapes=[pltpu.VMEM((tq, hw), F32),
                        pltpu.VMEM((heads, tq, 1), F32)],
        compiler_params=_params(("parallel", "parallel", "arbitrary")),
        name="stick_breaking",
    )(proj, proj, proj)


def _merge_kernel(a_ref, b_ref, pa_hbm, pb_hbm, ga_ref, gb_ref, o_ref,
                  sa_ref, sb_ref, pab_ref, pbb_ref, sem):
    @pl.when(pl.program_id(1) == 0)
    def _():
        j, n = pl.program_id(0), pl.num_programs(0)
        _load_column_tile(pa_hbm, sa_ref, pab_ref, sem.at[0], j, n)
        _load_column_tile(pb_hbm, sb_ref, pbb_ref, sem.at[1], j, n)

    cw = min(o_ref.shape[1], 256)
    for c in range(o_ref.shape[1] // cw):
        cols = slice(c * cw, (c + 1) * cw)
        ya = jnp.dot(a_ref[...], pab_ref[:, cols], preferred_element_type=F32)
        yb = jnp.dot(b_ref[...], pbb_ref[:, cols], preferred_element_type=F32)
        o_ref[:, cols] = (ga_ref[:, cols].astype(F32) * ya
                          + gb_ref[:, cols].astype(F32) * yb).astype(o_ref.dtype)


def _merge(a, b, proj, w_pa, w_pb, W, D):
    T = a.shape[0]
    tm = _tile(T, 512)
    tn = _tile(W, 1024)
    ga0 = 5 * W // tn
    gb0 = (5 * W + D) // tn
    return pl.pallas_call(
        _merge_kernel,
        grid=(D // tn, T // tm),
        in_specs=[pl.BlockSpec((tm, W), lambda j, i: (i, 0)),
                  pl.BlockSpec((tm, W), lambda j, i: (i, 0)),
                  pl.BlockSpec(memory_space=pl.ANY),
                  pl.BlockSpec(memory_space=pl.ANY),
                  pl.BlockSpec((tm, tn), lambda j, i: (i, ga0 + j)),
                  pl.BlockSpec((tm, tn), lambda j, i: (i, gb0 + j))],
        out_specs=pl.BlockSpec((tm, tn), lambda j, i: (i, j)),
        out_shape=jax.ShapeDtypeStruct((T, D), BF16),
        scratch_shapes=[pltpu.VMEM((W, tn), F32), pltpu.VMEM((W, tn), F32),
                        pltpu.VMEM((W, tn), BF16), pltpu.VMEM((W, tn), BF16),
                        pltpu.SemaphoreType.DMA((2,))],
        compiler_params=_params(("arbitrary", "arbitrary")),
        name="merge",
    )(a, b, w_pa, w_pb, proj, proj)


def _out_kernel(m_ref, w_hbm, x_ref, o_ref, stage_ref, wb_ref, sem):
    @pl.when(pl.program_id(1) == 0)
    def _():
        _load_column_tile(w_hbm, stage_ref, wb_ref, sem, pl.program_id(0), pl.num_programs(0))

    cw = min(o_ref.shape[1], 256)
    for c in range(o_ref.shape[1] // cw):
        cols = slice(c * cw, (c + 1) * cw)
        o_ref[:, cols] = x_ref[:, cols] + jnp.dot(m_ref[...], wb_ref[:, cols],
                                                  preferred_element_type=F32)


def _out_proj(merged, w_out, x):
    T, D = x.shape
    tm = _tile(T, 512)
    tn = _tile(D, 1024)
    return pl.pallas_call(
        _out_kernel,
        grid=(D // tn, T // tm),
        in_specs=[pl.BlockSpec((tm, D), lambda j, i: (i, 0)),
                  pl.BlockSpec(memory_space=pl.ANY),
                  pl.BlockSpec((tm, tn), lambda j, i: (i, j))],
        out_specs=pl.BlockSpec((tm, tn), lambda j, i: (i, j)),
        out_shape=jax.ShapeDtypeStruct((T, D), F32),
        scratch_shapes=[pltpu.VMEM((D, tn), F32), pltpu.VMEM((D, tn), BF16),
                        pltpu.SemaphoreType.DMA(())],
        compiler_params=_params(("arbitrary", "arbitrary")),
        name="out_proj",
    )(merged, w_out, x)


def _route_kernel(x_ref, g_ref, wr_ref, br_ref, xn_ref, e_ref, p_ref, c_ref):
    xn = _rms(x_ref[...], g_ref[...])
    xn_ref[...] = xn
    wr = wr_ref[...]
    wr_hi = wr.astype(BF16)
    wr_lo = (wr - wr_hi.astype(F32)).astype(BF16)
    xn_hi = xn.astype(BF16)
    xn_lo = (xn - xn_hi.astype(F32)).astype(BF16)
    logits = (jnp.dot(xn_hi, wr_hi, preferred_element_type=F32)
              + jnp.dot(xn_lo, wr_hi, preferred_element_type=F32)
              + jnp.dot(xn_hi, wr_lo, preferred_element_type=F32)) + br_ref[...]
    tm = logits.shape[0]
    lane = lax.broadcasted_iota(jnp.int32, logits.shape, 1)
    neg = -jnp.inf

    def top(vals):
        m = jnp.max(vals, axis=1, keepdims=True)
        idx = jnp.min(jnp.where(vals == m, lane, ROUTE_LANES), axis=1, keepdims=True)
        return m, idx

    is_group = lane < N_GROUPS_MOE
    gmax, grp = top(jnp.where(is_group, logits, neg))
    p_grp = 1.0 / jnp.sum(jnp.where(is_group, jnp.exp(logits - gmax), 0.0), axis=1, keepdims=True)
    first = N_GROUPS_MOE + grp * EXPERTS_PER_GROUP
    local = jnp.where((lane >= first) & (lane < first + EXPERTS_PER_GROUP), logits, neg)
    m1, i1 = top(local)
    m2, i2 = top(jnp.where(lane == i1, neg, local))
    e2 = jnp.exp(m2 - m1)
    w1 = p_grp / (1.0 + e2)
    w2 = p_grp * e2 / (1.0 + e2)
    ex1 = i1 - N_GROUPS_MOE
    ex2 = i2 - N_GROUPS_MOE

    hot1 = lane == ex1
    hot2 = lane == ex2
    chosen = (hot1 | hot2).astype(BF16)
    row = lax.broadcasted_iota(jnp.int32, (tm, tm), 0)
    col = lax.broadcasted_iota(jnp.int32, (tm, tm), 1)
    earlier = jnp.dot((col < row).astype(BF16), chosen, preferred_element_type=F32)
    r1 = jnp.sum(jnp.where(hot1, earlier, 0.0), axis=1, keepdims=True).astype(jnp.int32)
    r2 = jnp.sum(jnp.where(hot2, earlier, 0.0), axis=1, keepdims=True).astype(jnp.int32)
    counts = jnp.sum(chosen.astype(F32), axis=0, keepdims=True).astype(jnp.int32)

    e_ref[...] = jnp.where(lane == 0, ex1, jnp.where(lane == 1, ex2,
                           jnp.where(lane == 2, r1, jnp.where(lane == 3, r2, 0))))
    p_ref[...] = jnp.where(lane == 0, w1, jnp.where(lane == 1, w2, 0.0))
    c_ref[...] = jnp.broadcast_to(counts[None], c_ref.shape)


def _route(x1, g_ffn, w_group, b_group, w_router, b_router):
    T, D = x1.shape
    n_logit = w_group.shape[1] + w_router.shape[1]
    assert n_logit <= ROUTE_LANES
    wr = jnp.zeros((D, ROUTE_LANES), F32).at[:, :n_logit].set(jnp.concatenate([w_group, w_router], axis=1))
    br = jnp.zeros((1, ROUTE_LANES), F32).at[0, :n_logit].set(jnp.concatenate([b_group, b_router]))
    tm = _tile(T, 256)
    n_tiles = T // tm
    return tm, pl.pallas_call(
        _route_kernel,
        grid=(n_tiles,),
        in_specs=[pl.BlockSpec((tm, D), lambda i: (i, 0)),
                  pl.BlockSpec((1, D), lambda i: (0, 0)),
                  pl.BlockSpec((D, ROUTE_LANES), lambda i: (0, 0)),
                  pl.BlockSpec((1, ROUTE_LANES), lambda i: (0, 0))],
        out_specs=[pl.BlockSpec((tm, D), lambda i: (i, 0)),
                   pl.BlockSpec((tm, ROUTE_LANES), lambda i: (i, 0)),
                   pl.BlockSpec((tm, ROUTE_LANES), lambda i: (i, 0)),
                   pl.BlockSpec((1, 8, ROUTE_LANES), lambda i: (i, 0, 0))],
        out_shape=[jax.ShapeDtypeStruct((T, D), F32),
                   jax.ShapeDtypeStruct((T, ROUTE_LANES), jnp.int32),
                   jax.ShapeDtypeStruct((T, ROUTE_LANES), F32),
                   jax.ShapeDtypeStruct((n_tiles, 8, ROUTE_LANES), jnp.int32)],
        compiler_params=_params(("parallel",)),
        name="route",
    )(x1, g_ffn.reshape(1, D), wr, br)


def _dispatch_tables(experts, ranks, tile_counts, route_tm, n_exp, blk, n_blocks):
    T = experts.shape[0]
    i32 = jnp.int32
    tile_base = jnp.cumsum(tile_counts, axis=0) - tile_counts
    counts = jnp.sum(tile_counts, axis=0)
    padded = (counts + blk - 1) // blk * blk
    pad_end = jnp.cumsum(padded)
    pad_start = pad_end - padded
    base = jnp.repeat(tile_base + pad_start[None, :], route_tm, axis=0)
    hot = experts[:, :, None] == jnp.arange(n_exp, dtype=i32)[None, None, :]
    pos = (jnp.sum(jnp.where(hot, base[:, None, :], 0), axis=2) + ranks).astype(i32)
    token = jnp.broadcast_to(jnp.arange(T, dtype=i32)[:, None], pos.shape)
    pad_tok = jnp.arange(n_blocks * blk, dtype=i32) % T
    row_tok = pad_tok.at[pos.reshape(-1)].set(token.reshape(-1), unique_indices=True)

    n_used = pad_end[-1] // blk
    bstart = jnp.arange(n_blocks, dtype=i32) * blk
    be = jnp.minimum(jnp.sum((pad_end[None, :] <= bstart[:, None]).astype(i32), axis=1), n_exp - 1)
    be = jnp.where(jnp.arange(n_blocks) < n_used, be, be[n_used - 1])

    blk_id = jnp.arange(n_blocks, dtype=i32)
    blk_used = blk_id < n_used
    blk_first = blk_used & (blk_id == pad_start[be] // blk)
    nxt_blk = blk_id + jnp.maximum(padded[be] // blk, 1)
    blk_has_next = blk_first & (nxt_blk < n_used)
    nxt_blk = jnp.minimum(nxt_blk, n_blocks - 1)
    block_flags = (USED * blk_used + FIRST * blk_first + HAS_NEXT * blk_has_next).astype(i32)
    return pos, row_tok, (be.astype(i32), be[nxt_blk].astype(i32), block_flags)


def _dispatch_kernel(tok_hbm, x_hbm, o_ref, idx_smem, xbuf, idx_sem, row_sem, *, blk, n_blocks):
    i = pl.program_id(0)

    def idx_copy(b):
        return pltpu.make_async_copy(tok_hbm.at[pl.ds(pl.multiple_of(b * blk, blk), blk)],
                                     idx_smem.at[b % 2], idx_sem)

    def row_copy(tok, slot, r):
        return pltpu.make_async_copy(x_hbm.at[pl.ds(tok, 1), :], xbuf.at[slot, pl.ds(r, 1), :],
                                     row_sem.at[slot])

    def issue_rows(b):
        slot = b % 2

        def body(r, c):
            row_copy(idx_smem[slot, r], slot, r).start()
            return c
        lax.fori_loop(0, blk, body, 0, unroll=16)

    def wait_rows(slot):
        pltpu.make_async_copy(x_hbm.at[pl.ds(0, blk), :], xbuf.at[slot], row_sem.at[slot]).wait()

    @pl.when(i == 0)
    def _():
        idx_copy(0).start()
        idx_copy(0).wait()
        issue_rows(0)
        if n_blocks > 1:
            idx_copy(1).start()

    @pl.when(i + 1 < n_blocks)
    def _():
        idx_copy(i + 1).wait()
        issue_rows(i + 1)

    @pl.when(i + 2 < n_blocks)
    def _():
        idx_copy(i + 2).start()

    slot = i % 2
    wait_rows(slot)
    o_ref[...] = xbuf[slot].astype(o_ref.dtype)


def _dispatch(xn, row_tok, blk, n_blocks):
    T, D = xn.shape
    kern = functools.partial(_dispatch_kernel, blk=blk, n_blocks=n_blocks)
    return pl.pallas_call(
        kern,
        grid=(n_blocks,),
        in_specs=[pl.BlockSpec(memory_space=pl.ANY),
                  pl.BlockSpec(memory_space=pl.ANY)],
        out_specs=pl.BlockSpec((blk, D), lambda i: (i, 0)),
        out_shape=jax.ShapeDtypeStruct((n_blocks * blk, D), BF16),
        scratch_shapes=[pltpu.SMEM((2, blk), jnp.int32),
                        pltpu.VMEM((2, blk, D), F32),
                        pltpu.SemaphoreType.DMA(()),
                        pltpu.SemaphoreType.DMA((2,))],
        compiler_params=_params(("arbitrary",)),
        name="moe_dispatch",
    )(row_tok, xn)


def _moe_up_kernel(e_ref, next_e_ref, flag_ref, x_ref, wg_hbm, wu_hbm, o_ref,
                   sg_ref, su_ref, wgb_ref, wub_ref, sem):
    i = pl.program_id(0)
    flag = flag_ref[i]
    cw = min(o_ref.shape[1], 256)

    def copies(e):
        return (pltpu.make_async_copy(wg_hbm.at[e], sg_ref, sem.at[0]),
                pltpu.make_async_copy(wu_hbm.at[e], su_ref, sem.at[1]))

    def project(convert):
        for c in range(o_ref.shape[1] // cw):
            cols = slice(c * cw, (c + 1) * cw)
            if convert:
                wgb_ref[:, cols] = sg_ref[:, cols].astype(BF16)
                wub_ref[:, cols] = su_ref[:, cols].astype(BF16)
            g = jnp.dot(x_ref[...], wgb_ref[:, cols], preferred_element_type=F32)
            u = jnp.dot(x_ref[...], wub_ref[:, cols], preferred_element_type=F32)
            o_ref[:, cols] = (g * _sigmoid(g) * u).astype(o_ref.dtype)

    @pl.when((flag & FIRST) != 0)
    def _():
        @pl.when(i == 0)
        def _():
            for cp in copies(e_ref[0]):
                cp.start()

        for cp in copies(e_ref[i]):
            cp.wait()
        project(True)

        @pl.when((flag & HAS_NEXT) != 0)
        def _():
            for cp in copies(next_e_ref[i]):
                cp.start()

    @pl.when((flag & (USED | FIRST)) == USED)
    def _():
        project(False)

    @pl.when((flag & USED) == 0)
    def _():
        o_ref[...] = jnp.zeros_like(o_ref)


def _moe_up(xs, block_tables, w_gate, w_up, blk):
    n_rows, D = xs.shape
    hid = w_gate.shape[2]
    grid_spec = pltpu.PrefetchScalarGridSpec(
        num_scalar_prefetch=3,
        grid=(n_rows // blk,),
        in_specs=[pl.BlockSpec((blk, D), lambda i, e, ne, f: (i, 0)),
                  pl.BlockSpec(memory_space=pl.ANY),
                  pl.BlockSpec(memory_space=pl.ANY)],
        out_specs=pl.BlockSpec((blk, hid), lambda i, e, ne, f: (i, 0)),
        scratch_shapes=[pltpu.VMEM((D, hid), F32), pltpu.VMEM((D, hid), F32),
                        pltpu.VMEM((D, hid), BF16), pltpu.VMEM((D, hid), BF16),
                        pltpu.SemaphoreType.DMA((2,))],
    )
    return pl.pallas_call(
        _moe_up_kernel,
        grid_spec=grid_spec,
        out_shape=jax.ShapeDtypeStruct((n_rows, hid), BF16),
        compiler_params=_params(("arbitrary",)),
        name="moe_up",
    )(*block_tables, xs, w_gate, w_up)


def _moe_down_kernel(e_ref, next_e_ref, flag_ref, h_ref, wd_hbm, o_ref, stage_ref, wdb_ref, sem):
    i = pl.program_id(0)
    flag = flag_ref[i]
    cw = min(o_ref.shape[1], 256)

    def copy(e):
        return pltpu.make_async_copy(wd_hbm.at[e], stage_ref, sem)

    def project(convert):
        for c in range(o_ref.shape[1] // cw):
            cols = slice(c * cw, (c + 1) * cw)
            if convert:
                wdb_ref[:, cols] = stage_ref[:, cols].astype(BF16)
            o_ref[:, cols] = jnp.dot(h_ref[...], wdb_ref[:, cols], preferred_element_type=F32)

    @pl.when((flag & FIRST) != 0)
    def _():
        @pl.when(i == 0)
        def _():
            copy(e_ref[0]).start()

        copy(e_ref[i]).wait()
        project(True)

        @pl.when((flag & HAS_NEXT) != 0)
        def _():
            copy(next_e_ref[i]).start()

    @pl.when((flag & (USED | FIRST)) == USED)
    def _():
        project(False)

    @pl.when((flag & USED) == 0)
    def _():
        o_ref[...] = jnp.zeros_like(o_ref)


def _moe_down(hid, block_tables, w_down, blk):
    n_rows, H = hid.shape
    D = w_down.shape[2]
    grid_spec = pltpu.PrefetchScalarGridSpec(
        num_scalar_prefetch=3,
        grid=(n_rows // blk,),
        in_specs=[pl.BlockSpec((blk, H), lambda i, e, ne, f: (i, 0)),
                  pl.BlockSpec(memory_space=pl.ANY)],
        out_specs=pl.BlockSpec((blk, D), lambda i, e, ne, f: (i, 0)),
        scratch_shapes=[pltpu.VMEM((H, D), F32), pltpu.VMEM((H, D), BF16),
                        pltpu.SemaphoreType.DMA(())],
    )
    return pl.pallas_call(
        _moe_down_kernel,
        grid_spec=grid_spec,
        out_shape=jax.ShapeDtypeStruct((n_rows, D), F32),
        compiler_params=_params(("arbitrary",)),
        name="moe_down",
    )(*block_tables, hid, w_down)


def _combine_kernel(pos_hbm, y_hbm, x_ref, w_ref, g_ref, o_ref,
                    idx_smem, ybuf, idx_sem, row_sem, *, tm, n_tiles, final_norm):
    i = pl.program_id(0)
    n_idx = TOP_K * tm

    def idx_copy(t):
        return pltpu.make_async_copy(pos_hbm.at[pl.ds(pl.multiple_of(t * n_idx, n_idx), n_idx)],
                                     idx_smem.at[t % 2], idx_sem)

    def row_copy(src, slot, dst):
        return pltpu.make_async_copy(y_hbm.at[pl.ds(src, 1), :], ybuf.at[slot, pl.ds(dst, 1), :],
                                     row_sem.at[slot])

    def issue_rows(t):
        slot = t % 2

        def body(r, c):
            for k in range(TOP_K):
                row_copy(idx_smem[slot, TOP_K * r + k], slot, k * tm + r).start()
            return c
        lax.fori_loop(0, tm, body, 0, unroll=8)

    def wait_rows(slot):
        pltpu.make_async_copy(y_hbm.at[pl.ds(0, n_idx), :], ybuf.at[slot], row_sem.at[slot]).wait()

    @pl.when(i == 0)
    def _():
        idx_copy(0).start()
        idx_copy(0).wait()
        issue_rows(0)
        if n_tiles > 1:
            idx_copy(1).start()

    @pl.when(i + 1 < n_tiles)
    def _():
        idx_copy(i + 1).wait()
        issue_rows(i + 1)

    @pl.when(i + 2 < n_tiles)
    def _():
        idx_copy(i + 2).start()

    slot = i % 2
    wait_rows(slot)
    w = w_ref[...]
    out = x_ref[...] + w[:, 0:1] * ybuf[slot, 0:tm, :] + w[:, 1:2] * ybuf[slot, tm:2 * tm, :]
    if final_norm:
        out = _rms(out, g_ref[...])
    o_ref[...] = out


def _combine(x1, y, pos, w_assign, g_final, final_norm):
    T, D = x1.shape
    tm = _tile(T, 256)
    n_tiles = T // tm
    kern = functools.partial(_combine_kernel, tm=tm, n_tiles=n_tiles, final_norm=final_norm)
    return pl.pallas_call(
        kern,
        grid=(n_tiles,),
        in_specs=[pl.BlockSpec(memory_space=pl.ANY),
                  pl.BlockSpec(memory_space=pl.ANY),
                  pl.BlockSpec((tm, D), lambda i: (i, 0)),
                  pl.BlockSpec((tm, TOP_K), lambda i: (i, 0)),
                  pl.BlockSpec((1, D), lambda i: (0, 0))],
        out_specs=pl.BlockSpec((tm, D), lambda i: (i, 0)),
        out_shape=jax.ShapeDtypeStruct((T, D), F32),
        scratch_shapes=[pltpu.SMEM((2, TOP_K * tm), jnp.int32),
                        pltpu.VMEM((2, TOP_K * tm, D), F32),
                        pltpu.SemaphoreType.DMA(()),
                        pltpu.SemaphoreType.DMA((2,))],
        compiler_params=_params(("arbitrary",)),
        name="moe_combine",
    )(pos.reshape(-1), y, x1, w_assign, g_final.reshape(1, D))


def _layer(x, B, S, p, g_final, final_norm):
    T, D = x.shape
    W = D // 2
    xn = _norm_bf16(x, p["g_mix"])
    proj = _in_proj(xn, p["w_in"], p["b_gate"], W)
    a = _sgu(proj, p["g_sgu"], p["w_sgu"], p["b_sgu"], W)
    b = _attention(proj, B, S, W)
    merged = _merge(a, b, proj, p["w_proj_a"], p["w_proj_b"], W, D)
    x1 = _out_proj(merged, p["w_out"], x)

    route_tm, (xn2, e_out, p_out, tile_counts) = _route(
        x1, p["g_ffn"], p["w_group"], p["b_group"], p["w_router"], p["b_router"])
    n_exp = p["w_gate"].shape[0]
    blk = 256
    n_blocks = T * TOP_K // blk + n_exp
    pos, row_tok, block_tables = _dispatch_tables(
        e_out[:, :TOP_K], e_out[:, TOP_K:2 * TOP_K], tile_counts[:, 0, :n_exp],
        route_tm, n_exp, blk, n_blocks)
    xs = _dispatch(xn2, row_tok, blk, n_blocks)
    hid = _moe_up(xs, block_tables, p["w_gate"], p["w_up"], blk)
    y = _moe_down(hid, block_tables, p["w_down"], blk)
    return _combine(x1, y, pos, p_out[:, :TOP_K], g_final, final_norm)


def kernel(x, g_mix, w_in, g_sgu, w_sgu, b_sgu, b_gate, w_proj_a, w_proj_b, w_out,
           g_ffn, w_group, b_group, w_router, b_router, w_gate, w_up, w_down, g_final):
    B, S, D = x.shape
    stacked = dict(g_mix=g_mix, w_in=w_in, g_sgu=g_sgu, w_sgu=w_sgu, b_sgu=b_sgu, b_gate=b_gate,
                   w_proj_a=w_proj_a, w_proj_b=w_proj_b, w_out=w_out, g_ffn=g_ffn,
                   w_group=w_group, b_group=b_group, w_router=w_router, b_router=b_router,
                   w_gate=w_gate, w_up=w_up, w_down=w_down)
    depth = w_in.shape[0]
    xf = x.reshape(B * S, D)
    for l in range(depth):
        p = {name: val[l] for name, val in stacked.items()}
        xf = _layer(xf, B, S, p, g_final, final_norm=(l == depth - 1))
    return xf.reshape(B, S, D)
```

```python
import functools

import jax
import jax.numpy as jnp
from jax import lax
from jax.experimental import pallas as pl
from jax.experimental.pallas import tpu as pltpu

F32 = jnp.float32
BF16 = jnp.bfloat16

EPS = 1e-6
CHUNK = 64
SGU_BLOCK = 128
SGU_GROUPS = 8
HEAD_DIM = 128
N_GROUPS_MOE = 4
EXPERTS_PER_GROUP = 8
TOP_K = 2
ROUTE_LANES = 128
USED, FIRST, HAS_NEXT = 1, 2, 4

SKIP_BELOW = 120.0

VMEM_LIMIT = 56 * 2**20


def _tile(n, pref):
    t = min(n, pref)
    assert n % t == 0, (n, pref)
    return t


def _params(sem):
    return pltpu.CompilerParams(dimension_semantics=sem, vmem_limit_bytes=VMEM_LIMIT)


def _sigmoid(x):
    return 0.5 * (1.0 + jnp.tanh(0.5 * x))


def _rms(x, g):
    return x * lax.rsqrt(jnp.mean(x * x, axis=-1, keepdims=True) + EPS) * g


def _norm_kernel(x_ref, g_ref, o_ref):
    o_ref[...] = _rms(x_ref[...], g_ref[...]).astype(o_ref.dtype)


def _norm_bf16(x, g):
    T, D = x.shape
    tm = _tile(T, 512)
    return pl.pallas_call(
        _norm_kernel,
        grid=(T // tm,),
        in_specs=[pl.BlockSpec((tm, D), lambda i: (i, 0)),
                  pl.BlockSpec((1, D), lambda i: (0, 0))],
        out_specs=pl.BlockSpec((tm, D), lambda i: (i, 0)),
        out_shape=jax.ShapeDtypeStruct((T, D), BF16),
        compiler_params=_params(("parallel",)),
        name="norm_bf16",
    )(x, g.reshape(1, D))


def _load_column_tile(w_hbm, stage_ref, wb_ref, sem, j, n_tiles):
    tn = stage_ref.shape[1]

    def copy(t):
        return pltpu.make_async_copy(w_hbm.at[:, pl.ds(pl.multiple_of(t * tn, tn), tn)], stage_ref, sem)

    @pl.when(j == 0)
    def _():
        copy(0).start()

    copy(j).wait()
    wb_ref[...] = stage_ref[...].astype(BF16)

    @pl.when(j + 1 < n_tiles)
    def _():
        copy(j + 1).start()


def _proj_kernel(x_ref, w_hbm, c_ref, o_ref, stage_ref, wb_ref, sem, *, n_gelu, n_plain_end):
    j = pl.program_id(0)

    @pl.when(pl.program_id(1) == 0)
    def _():
        _load_column_tile(w_hbm, stage_ref, wb_ref, sem, j, pl.num_programs(0))

    def project(epilogue):
        cw = min(o_ref.shape[1], 256)
        for c in range(o_ref.shape[1] // cw):
            cols = slice(c * cw, (c + 1) * cw)
            y = jnp.dot(x_ref[...], wb_ref[:, cols], preferred_element_type=F32)
            o_ref[:, cols] = epilogue(y, c_ref[0:1, cols]).astype(o_ref.dtype)

    def gelu_tanh(y, _):
        c1 = 0.7978845608028654
        return (0.5 * y) * (1.0 + jnp.tanh(y * (c1 + (c1 * 0.044715) * (y * y))))

    @pl.when(j < n_gelu)
    def _():
        project(gelu_tanh)

    @pl.when((j >= n_gelu) & (j < n_plain_end))
    def _():
        project(lambda y, scale: y * scale)

    @pl.when(j >= n_plain_end)
    def _():
        project(lambda y, half_b: 0.5 + 0.5 * jnp.tanh(0.5 * y + half_b))


def _proj_epilogue_row(b_gate, W):
    seg = lambda v, n: jnp.full((n,), v, F32)
    row = jnp.concatenate([seg(0.0, 2 * W), seg(HEAD_DIM ** -0.5, W), seg(1.0, 2 * W),
                           0.5 * b_gate.reshape(-1).astype(F32)])
    return jnp.broadcast_to(row[None, :], (8, row.shape[0]))


def _in_proj(xn, w_in, b_gate, W):
    T, D = xn.shape
    N = w_in.shape[1]
    tm = _tile(T, 1024)
    tn = _tile(W, 1024)
    coef = _proj_epilogue_row(b_gate, W)
    assert coef.shape == (8, N)
    kern = functools.partial(_proj_kernel, n_gelu=2 * W // tn, n_plain_end=5 * W // tn)
    return pl.pallas_call(
        kern,
        grid=(N // tn, T // tm),
        in_specs=[pl.BlockSpec((tm, D), lambda j, i: (i, 0)),
                  pl.BlockSpec(memory_space=pl.ANY),
                  pl.BlockSpec((8, tn), lambda j, i: (0, j))],
        out_specs=pl.BlockSpec((tm, tn), lambda j, i: (i, j)),
        out_shape=jax.ShapeDtypeStruct((T, N), BF16),
        scratch_shapes=[pltpu.VMEM((D, tn), F32), pltpu.VMEM((D, tn), BF16),
                        pltpu.SemaphoreType.DMA(())],
        compiler_params=_params(("arbitrary", "arbitrary")),
        name="in_proj",
    )(xn, w_in, coef)


def _sgu_kernel(u_ref, v_ref, g_ref, w_ref, bt_ref, o_ref, *, n_sub, gd):
    v = v_ref[...].astype(F32)
    vn = _rms(v, g_ref[...]).astype(BF16)
    t_chunk = lax.broadcasted_iota(jnp.int32, (SGU_BLOCK, SGU_BLOCK), 0) // CHUNK
    s_chunk = lax.broadcasted_iota(jnp.int32, (SGU_BLOCK, SGU_BLOCK), 1) // CHUNK
    causal = s_chunk <= t_chunk
    for g in range(SGU_GROUPS):
        w = jnp.where(causal, w_ref[g], 0.0).astype(BF16)
        bias = bt_ref[:, g:g + 1]
        cols = slice(g * gd, (g + 1) * gd)
        for n in range(n_sub):
            rows = slice(n * SGU_BLOCK, (n + 1) * SGU_BLOCK)
            mixed = jnp.dot(w, vn[rows, cols], preferred_element_type=F32) + bias
            o_ref[rows, cols] = (u_ref[rows, cols].astype(F32) * mixed).astype(o_ref.dtype)


def _sgu(proj, g_sgu, w_sgu, b_sgu, W):
    T = proj.shape[0]
    tm = _tile(T, 2 * SGU_BLOCK)
    gd = W // SGU_GROUPS
    kern = functools.partial(_sgu_kernel, n_sub=tm // SGU_BLOCK, gd=gd)
    return pl.pallas_call(
        kern,
        grid=(T // tm,),
        in_specs=[pl.BlockSpec((tm, W), lambda i: (i, 0)),
                  pl.BlockSpec((tm, W), lambda i: (i, 1)),
                  pl.BlockSpec((1, W), lambda i: (0, 0)),
                  pl.BlockSpec((SGU_GROUPS, SGU_BLOCK, SGU_BLOCK), lambda i: (0, 0, 0)),
                  pl.BlockSpec((SGU_BLOCK, SGU_GROUPS), lambda i: (0, 0))],
        out_specs=pl.BlockSpec((tm, W), lambda i: (i, 0)),
        out_shape=jax.ShapeDtypeStruct((T, W), BF16),
        compiler_params=_params(("parallel",)),
        name="sgu",
    )(proj, proj, g_sgu.reshape(1, W), w_sgu, b_sgu.T)


def _attn_kernel(q_ref, k_ref, v_ref, o_ref, acc_ref, carry_ref, *, tq, heads):
    qi = pl.program_id(2)
    jj = lax.broadcasted_iota(jnp.int32, (tq, tq), 0)
    ss = lax.broadcasted_iota(jnp.int32, (tq, tq), 1)
    tri = (jj >= ss).astype(BF16)
    before = ss < jj

    def step(kb, diagonal):
        rows = pl.ds(pl.multiple_of(kb * tq, tq), tq)
        hcols = [slice(h * HEAD_DIM, (h + 1) * HEAD_DIM) for h in range(heads)]
        zs = [lax.dot_general(q_ref[:, c], k_ref[rows, c], (((1,), (1,)), ((), ())),
                              preferred_element_type=F32) for c in hcols]
        logs = []
        for z in zs:
            lg = -(jnp.maximum(z, 0.0) + jnp.log(1.0 + jnp.exp(-jnp.abs(z))))
            if diagonal:
                lg = jnp.where(before, lg, 0.0)
            logs.append(lg.astype(BF16))
        incls = [jnp.dot(lg, tri, preferred_element_type=F32) for lg in logs]
        weights, worst = [], None
        for h, (z, incl) in enumerate(zip(zs, incls)):
            if diagonal:
                a = jnp.where(before, jnp.exp(z + incl), 0.0)
                carry = incl[:, 0:1]
            else:
                a = jnp.exp(z + incl + carry_ref[h])
                carry = carry_ref[h] + incl[:, 0:1]
            carry_ref[h] = carry
            weights.append(a.astype(BF16))
            worst = carry if worst is None else jnp.maximum(worst, carry)
        for c, a in zip(hcols, weights):
            pv = jnp.dot(a, v_ref[rows, c], preferred_element_type=F32)
            if diagonal:
                acc_ref[:, c] = pv
            else:
                acc_ref[:, c] += pv
        return jnp.max(worst)

    def cond(state):
        kb, worst = state
        return (kb >= 0) & (worst > -SKIP_BELOW)

    def body(state):
        kb, _ = state
        return kb - 1, step(kb, False)

    lax.while_loop(cond, body, (qi - 1, step(qi, True)))
    o_ref[...] = acc_ref[...].astype(o_ref.dtype)


def _attention(proj, B, S, W):
    T = proj.shape[0]
    n_heads = W // HEAD_DIM
    heads = min(4, n_heads)
    assert n_heads % heads == 0
    hw = heads * HEAD_DIM
    tq = _tile(S, 256)
    nq = S // tq
    q0, k0, v0 = 2 * W // hw, 3 * W // hw, 4 * W // hw
    kern = functools.partial(_attn_kernel, tq=tq, heads=heads)
    return pl.pallas_call(
        kern,
        grid=(B, n_heads // heads, nq),
        in_specs=[pl.BlockSpec((tq, hw), lambda b, g, i: (b * nq + i, q0 + g)),
                  pl.BlockSpec((S, hw), lambda b, g, i: (b, k0 + g)),
                  pl.BlockSpec((S, hw), lambda b, g, i: (b, v0 + g))],
        out_specs=pl.BlockSpec((tq, hw), lambda b, g, i: (b * nq + i, g)),
        out_shape=jax.ShapeDtypeStruct((T, W), BF16),
        scratch_shapes=[pltpu.VMEM((tq, hw), F32),
                        pltpu.VMEM((heads, tq, 1), F32)],
        compiler_params=_params(("parallel", "parallel", "arbitrary")),
        name="stick_breaking",
    )(proj, proj, proj)


def _merge_kernel(a_ref, b_ref, pa_hbm, pb_hbm, ga_ref, gb_ref, o_ref,
                  sa_ref, sb_ref, pab_ref, pbb_ref, sem):
    @pl.when(pl.program_id(1) == 0)
    def _():
        j, n = pl.program_id(0), pl.num_programs(0)
        _load_column_tile(pa_hbm, sa_ref, pab_ref, sem.at[0], j, n)
        _load_column_tile(pb_hbm, sb_ref, pbb_ref, sem.at[1], j, n)

    cw = min(o_ref.shape[1], 256)
    for c in range(o_ref.shape[1] // cw):
        cols = slice(c * cw, (c + 1) * cw)
        ya = jnp.dot(a_ref[...], pab_ref[:, cols], preferred_element_type=F32)
        yb = jnp.dot(b_ref[...], pbb_ref[:, cols], preferred_element_type=F32)
        o_ref[:, cols] = (ga_ref[:, cols].astype(F32) * ya
                          + gb_ref[:, cols].astype(F32) * yb).astype(o_ref.dtype)


def _merge(a, b, proj, w_pa, w_pb, W, D):
    T = a.shape[0]
    tm = _tile(T, 512)
    tn = _tile(W, 1024)
    ga0 = 5 * W // tn
    gb0 = (5 * W + D) // tn
    return pl.pallas_call(
        _merge_kernel,
        grid=(D // tn, T // tm),
        in_specs=[pl.BlockSpec((tm, W), lambda j, i: (i, 0)),
                  pl.BlockSpec((tm, W), lambda j, i: (i, 0)),
                  pl.BlockSpec(memory_space=pl.ANY),
                  pl.BlockSpec(memory_space=pl.ANY),
                  pl.BlockSpec((tm, tn), lambda j, i: (i, ga0 + j)),
                  pl.BlockSpec((tm, tn), lambda j, i: (i, gb0 + j))],
        out_specs=pl.BlockSpec((tm, tn), lambda j, i: (i, j)),
        out_shape=jax.ShapeDtypeStruct((T, D), BF16),
        scratch_shapes=[pltpu.VMEM((W, tn), F32), pltpu.VMEM((W, tn), F32),
                        pltpu.VMEM((W, tn), BF16), pltpu.VMEM((W, tn), BF16),
                        pltpu.SemaphoreType.DMA((2,))],
        compiler_params=_params(("arbitrary", "arbitrary")),
        name="merge",
    )(a, b, w_pa, w_pb, proj, proj)


def _out_kernel(m_ref, w_hbm, x_ref, o_ref, stage_ref, wb_ref, sem):
    @pl.when(pl.program_id(1) == 0)
    def _():
        _load_column_tile(w_hbm, stage_ref, wb_ref, sem, pl.program_id(0), pl.num_programs(0))

    cw = min(o_ref.shape[1], 256)
    for c in range(o_ref.shape[1] // cw):
        cols = slice(c * cw, (c + 1) * cw)
        o_ref[:, cols] = x_ref[:, cols] + jnp.dot(m_ref[...], wb_ref[:, cols],
                                                  preferred_element_type=F32)


def _out_proj(merged, w_out, x):
    T, D = x.shape
    tm = _tile(T, 512)
    tn = _tile(D, 1024)
    return pl.pallas_call(
        _out_kernel,
        grid=(D // tn, T // tm),
        in_specs=[pl.BlockSpec((tm, D), lambda j, i: (i, 0)),
                  pl.BlockSpec(memory_space=pl.ANY),
                  pl.BlockSpec((tm, tn), lambda j, i: (i, j))],
        out_specs=pl.BlockSpec((tm, tn), lambda j, i: (i, j)),
        out_shape=jax.ShapeDtypeStruct((T, D), F32),
        scratch_shapes=[pltpu.VMEM((D, tn), F32), pltpu.VMEM((D, tn), BF16),
                        pltpu.SemaphoreType.DMA(())],
        compiler_params=_params(("arbitrary", "arbitrary")),
        name="out_proj",
    )(merged, w_out, x)


def _route_kernel(x_ref, g_ref, wr_ref, br_ref, xn_ref, e_ref, p_ref, c_ref):
    xn = _rms(x_ref[...], g_ref[...])
    xn_ref[...] = xn
    wr = wr_ref[...]
    wr_hi = wr.astype(BF16)
    wr_lo = (wr - wr_hi.astype(F32)).astype(BF16)
    xn_hi = xn.astype(BF16)
    xn_lo = (xn - xn_hi.astype(F32)).astype(BF16)
    logits = (jnp.dot(xn_hi, wr_hi, preferred_element_type=F32)
              + jnp.dot(xn_lo, wr_hi, preferred_element_type=F32)
              + jnp.dot(xn_hi, wr_lo, preferred_element_type=F32)) + br_ref[...]
    tm = logits.shape[0]
    lane = lax.broadcasted_iota(jnp.int32, logits.shape, 1)
    neg = -jnp.inf

    def top(vals):
        m = jnp.max(vals, axis=1, keepdims=True)
        idx = jnp.min(jnp.where(vals == m, lane, ROUTE_LANES), axis=1, keepdims=True)
        return m, idx

    is_group = lane < N_GROUPS_MOE
    gmax, grp = top(jnp.where(is_group, logits, neg))
    p_grp = 1.0 / jnp.sum(jnp.where(is_group, jnp.exp(logits - gmax), 0.0), axis=1, keepdims=True)
    first = N_GROUPS_MOE + grp * EXPERTS_PER_GROUP
    local = jnp.where((lane >= first) & (lane < first + EXPERTS_PER_GROUP), logits, neg)
    m1, i1 = top(local)
    m2, i2 = top(jnp.where(lane == i1, neg, local))
    e2 = jnp.exp(m2 - m1)
    w1 = p_grp / (1.0 + e2)
    w2 = p_grp * e2 / (1.0 + e2)
    ex1 = i1 - N_GROUPS_MOE
    ex2 = i2 - N_GROUPS_MOE

    hot1 = lane == ex1
    hot2 = lane == ex2
    chosen = (hot1 | hot2).astype(BF16)
    row = lax.broadcasted_iota(jnp.int32, (tm, tm), 0)
    col = lax.broadcasted_iota(jnp.int32, (tm, tm), 1)
    earlier = jnp.dot((col < row).astype(BF16), chosen, preferred_element_type=F32)
    r1 = jnp.sum(jnp.where(hot1, earlier, 0.0), axis=1, keepdims=True).astype(jnp.int32)
    r2 = jnp.sum(jnp.where(hot2, earlier, 0.0), axis=1, keepdims=True).astype(jnp.int32)
    counts = jnp.sum(chosen.astype(F32), axis=0, keepdims=True).astype(jnp.int32)

    e_ref[...] = jnp.where(lane == 0, ex1, jnp.where(lane == 1, ex2,
                           jnp.where(lane == 2, r1, jnp.where(lane == 3, r2, 0))))
    p_ref[...] = jnp.where(lane == 0, w1, jnp.where(lane == 1, w2, 0.0))
    c_ref[...] = jnp.broadcast_to(counts[None], c_ref.shape)


def _route(x1, g_ffn, w_group, b_group, w_router, b_router):
    T, D = x1.shape
    n_logit = w_group.shape[1] + w_router.shape[1]
    assert n_logit <= ROUTE_LANES
    wr = jnp.zeros((D, ROUTE_LANES), F32).at[:, :n_logit].set(jnp.concatenate([w_group, w_router], axis=1))
    br = jnp.zeros((1, ROUTE_LANES), F32).at[0, :n_logit].set(jnp.concatenate([b_group, b_router]))
    tm = _tile(T, 256)
    n_tiles = T // tm
    return tm, pl.pallas_call(
        _route_kernel,
        grid=(n_tiles,),
        in_specs=[pl.BlockSpec((tm, D), lambda i: (i, 0)),
                  pl.BlockSpec((1, D), lambda i: (0, 0)),
                  pl.BlockSpec((D, ROUTE_LANES), lambda i: (0, 0)),
                  pl.BlockSpec((1, ROUTE_LANES), lambda i: (0, 0))],
        out_specs=[pl.BlockSpec((tm, D), lambda i: (i, 0)),
                   pl.BlockSpec((tm, ROUTE_LANES), lambda i: (i, 0)),
                   pl.BlockSpec((tm, ROUTE_LANES), lambda i: (i, 0)),
                   pl.BlockSpec((1, 8, ROUTE_LANES), lambda i: (i, 0, 0))],
        out_shape=[jax.ShapeDtypeStruct((T, D), F32),
                   jax.ShapeDtypeStruct((T, ROUTE_LANES), jnp.int32),
                   jax.ShapeDtypeStruct((T, ROUTE_LANES), F32),
                   jax.ShapeDtypeStruct((n_tiles, 8, ROUTE_LANES), jnp.int32)],
        compiler_params=_params(("parallel",)),
        name="route",
    )(x1, g_ffn.reshape(1, D), wr, br)


def _dispatch_tables(experts, ranks, tile_counts, route_tm, n_exp, blk, n_blocks):
    T = experts.shape[0]
    i32 = jnp.int32
    tile_base = jnp.cumsum(tile_counts, axis=0) - tile_counts
    counts = jnp.sum(tile_counts, axis=0)
    padded = (counts + blk - 1) // blk * blk
    pad_end = jnp.cumsum(padded)
    pad_start = pad_end - padded
    base = jnp.repeat(tile_base + pad_start[None, :], route_tm, axis=0)
    hot = experts[:, :, None] == jnp.arange(n_exp, dtype=i32)[None, None, :]
    pos = (jnp.sum(jnp.where(hot, base[:, None, :], 0), axis=2) + ranks).astype(i32)
    token = jnp.broadcast_to(jnp.arange(T, dtype=i32)[:, None], pos.shape)
    pad_tok = jnp.arange(n_blocks * blk, dtype=i32) % T
    row_tok = pad_tok.at[pos.reshape(-1)].set(token.reshape(-1), unique_indices=True)

    n_used = pad_end[-1] // blk
    bstart = jnp.arange(n_blocks, dtype=i32) * blk
    be = jnp.minimum(jnp.sum((pad_end[None, :] <= bstart[:, None]).astype(i32), axis=1), n_exp - 1)
    be = jnp.where(jnp.arange(n_blocks) < n_used, be, be[n_used - 1])

    blk_id = jnp.arange(n_blocks, dtype=i32)
    blk_used = blk_id < n_used
    blk_first = blk_used & (blk_id == pad_start[be] // blk)
    nxt_blk = blk_id + jnp.maximum(padded[be] // blk, 1)
    blk_has_next = blk_first & (nxt_blk < n_used)
    nxt_blk = jnp.minimum(nxt_blk, n_blocks - 1)
    block_flags = (USED * blk_used + FIRST * blk_first + HAS_NEXT * blk_has_next).astype(i32)
    return pos, row_tok, (be.astype(i32), be[nxt_blk].astype(i32), block_flags)


def _dispatch_kernel(tok_hbm, x_hbm, o_ref, idx_smem, xbuf, idx_sem, row_sem, *, blk, n_blocks):
    i = pl.program_id(0)

    def idx_copy(b):
        return pltpu.make_async_copy(tok_hbm.at[pl.ds(pl.multiple_of(b * blk, blk), blk)],
                                     idx_smem.at[b % 2], idx_sem)

    def row_copy(tok, slot, r):
        return pltpu.make_async_copy(x_hbm.at[pl.ds(tok, 1), :], xbuf.at[slot, pl.ds(r, 1), :],
                                     row_sem.at[slot])

    def issue_rows(b):
        slot = b % 2

        def body(r, c):
            row_copy(idx_smem[slot, r], slot, r).start()
            return c
        lax.fori_loop(0, blk, body, 0, unroll=16)

    def wait_rows(slot):
        pltpu.make_async_copy(x_hbm.at[pl.ds(0, blk), :], xbuf.at[slot], row_sem.at[slot]).wait()

    @pl.when(i == 0)
    def _():
        idx_copy(0).start()
        idx_copy(0).wait()
        issue_rows(0)
        if n_blocks > 1:
            idx_copy(1).start()

    @pl.when(i + 1 < n_blocks)
    def _():
        idx_copy(i + 1).wait()
        issue_rows(i + 1)

    @pl.when(i + 2 < n_blocks)
    def _():
        idx_copy(i + 2).start()

    slot = i % 2
    wait_rows(slot)
    o_ref[...] = xbuf[slot].astype(o_ref.dtype)


def _dispatch(xn, row_tok, blk, n_blocks):
    T, D = xn.shape
    kern = functools.partial(_dispatch_kernel, blk=blk, n_blocks=n_blocks)
    return pl.pallas_call(
        kern,
        grid=(n_blocks,),
        in_specs=[pl.BlockSpec(memory_space=pl.ANY),
                  pl.BlockSpec(memory_space=pl.ANY)],
        out_specs=pl.BlockSpec((blk, D), lambda i: (i, 0)),
        out_shape=jax.ShapeDtypeStruct((n_blocks * blk, D), BF16),
        scratch_shapes=[pltpu.SMEM((2, blk), jnp.int32),
                        pltpu.VMEM((2, blk, D), F32),
                        pltpu.SemaphoreType.DMA(()),
                        pltpu.SemaphoreType.DMA((2,))],
        compiler_params=_params(("arbitrary",)),
        name="moe_dispatch",
    )(row_tok, xn)


def _moe_up_kernel(e_ref, next_e_ref, flag_ref, x_ref, wg_hbm, wu_hbm, o_ref,
                   sg_ref, su_ref, wgb_ref, wub_ref, sem):
    i = pl.program_id(0)
    flag = flag_ref[i]
    cw = min(o_ref.shape[1], 256)

    def copies(e):
        return (pltpu.make_async_copy(wg_hbm.at[e], sg_ref, sem.at[0]),
                pltpu.make_async_copy(wu_hbm.at[e], su_ref, sem.at[1]))

    def project(convert):
        for c in range(o_ref.shape[1] // cw):
            cols = slice(c * cw, (c + 1) * cw)
            if convert:
                wgb_ref[:, cols] = sg_ref[:, cols].astype(BF16)
                wub_ref[:, cols] = su_ref[:, cols].astype(BF16)
            g = jnp.dot(x_ref[...], wgb_ref[:, cols], preferred_element_type=F32)
            u = jnp.dot(x_ref[...], wub_ref[:, cols], preferred_element_type=F32)
            o_ref[:, cols] = (g * _sigmoid(g) * u).astype(o_ref.dtype)

    @pl.when((flag & FIRST) != 0)
    def _():
        @pl.when(i == 0)
        def _():
            for cp in copies(e_ref[0]):
                cp.start()

        for cp in copies(e_ref[i]):
            cp.wait()
        project(True)

        @pl.when((flag & HAS_NEXT) != 0)
        def _():
            for cp in copies(next_e_ref[i]):
                cp.start()

    @pl.when((flag & (USED | FIRST)) == USED)
    def _():
        project(False)

    @pl.when((flag & USED) == 0)
    def _():
        o_ref[...] = jnp.zeros_like(o_ref)


def _moe_up(xs, block_tables, w_gate, w_up, blk):
    n_rows, D = xs.shape
    hid = w_gate.shape[2]
    grid_spec = pltpu.PrefetchScalarGridSpec(
        num_scalar_prefetch=3,
        grid=(n_rows // blk,),
        in_specs=[pl.BlockSpec((blk, D), lambda i, e, ne, f: (i, 0)),
                  pl.BlockSpec(memory_space=pl.ANY),
                  pl.BlockSpec(memory_space=pl.ANY)],
        out_specs=pl.BlockSpec((blk, hid), lambda i, e, ne, f: (i, 0)),
        scratch_shapes=[pltpu.VMEM((D, hid), F32), pltpu.VMEM((D, hid), F32),
                        pltpu.VMEM((D, hid), BF16), pltpu.VMEM((D, hid), BF16),
                        pltpu.SemaphoreType.DMA((2,))],
    )
    return pl.pallas_call(
        _moe_up_kernel,
        grid_spec=grid_spec,
        out_shape=jax.ShapeDtypeStruct((n_rows, hid), BF16),
        compiler_params=_params(("arbitrary",)),
        name="moe_up",
    )(*block_tables, xs, w_gate, w_up)


def _moe_down_kernel(e_ref, next_e_ref, flag_ref, h_ref, wd_hbm, o_ref, stage_ref, wdb_ref, sem):
    i = pl.program_id(0)
    flag = flag_ref[i]
    cw = min(o_ref.shape[1], 256)

    def copy(e):
        return pltpu.make_async_copy(wd_hbm.at[e], stage_ref, sem)

    def project(convert):
        for c in range(o_ref.shape[1] // cw):
            cols = slice(c * cw, (c + 1) * cw)
            if convert:
                wdb_ref[:, cols] = stage_ref[:, cols].astype(BF16)
            o_ref[:, cols] = jnp.dot(h_ref[...], wdb_ref[:, cols], preferred_element_type=F32)

    @pl.when((flag & FIRST) != 0)
    def _():
        @pl.when(i == 0)
        def _():
            copy(e_ref[0]).start()

        copy(e_ref[i]).wait()
        project(True)

        @pl.when((flag & HAS_NEXT) != 0)
        def _():
            copy(next_e_ref[i]).start()

    @pl.when((flag & (USED | FIRST)) == USED)
    def _():
        project(False)

    @pl.when((flag & USED) == 0)
    def _():
        o_ref[...] = jnp.zeros_like(o_ref)


def _moe_down(hid, block_tables, w_down, blk):
    n_rows, H = hid.shape
    D = w_down.shape[2]
    grid_spec = pltpu.PrefetchScalarGridSpec(
        num_scalar_prefetch=3,
        grid=(n_rows // blk,),
        in_specs=[pl.BlockSpec((blk, H), lambda i, e, ne, f: (i, 0)),
                  pl.BlockSpec(memory_space=pl.ANY)],
        out_specs=pl.BlockSpec((blk, D), lambda i, e, ne, f: (i, 0)),
        scratch_shapes=[pltpu.VMEM((H, D), F32), pltpu.VMEM((H, D), BF16),
                        pltpu.SemaphoreType.DMA(())],
    )
    return pl.pallas_call(
        _moe_down_kernel,
        grid_spec=grid_spec,
        out_shape=jax.ShapeDtypeStruct((n_rows, D), F32),
        compiler_params=_params(("arbitrary",)),
        name="moe_down",
    )(*block_tables, hid, w_down)


def _combine_kernel(pos_hbm, y_hbm, x_ref, w_ref, g_ref, o_ref,
                    idx_smem, ybuf, idx_sem, row_sem, *, tm, n_tiles, final_norm):
    i = pl.program_id(0)
    n_idx = TOP_K * tm

    def idx_copy(t):
        return pltpu.make_async_copy(pos_hbm.at[pl.ds(pl.multiple_of(t * n_idx, n_idx), n_idx)],
                                     idx_smem.at[t % 2], idx_sem)

    def row_copy(src, slot, dst):
        return pltpu.make_async_copy(y_hbm.at[pl.ds(src, 1), :], ybuf.at[slot, pl.ds(dst, 1), :],
                                     row_sem.at[slot])

    def issue_rows(t):
        slot = t % 2

        def body(r, c):
            for k in range(TOP_K):
                row_copy(idx_smem[slot, TOP_K * r + k], slot, k * tm + r).start()
            return c
        lax.fori_loop(0, tm, body, 0, unroll=8)

    def wait_rows(slot):
        pltpu.make_async_copy(y_hbm.at[pl.ds(0, n_idx), :], ybuf.at[slot], row_sem.at[slot]).wait()

    @pl.when(i == 0)
    def _():
        idx_copy(0).start()
        idx_copy(0).wait()
        issue_rows(0)
        if n_tiles > 1:
            idx_copy(1).start()

    @pl.when(i + 1 < n_tiles)
    def _():
        idx_copy(i + 1).wait()
        issue_rows(i + 1)

    @pl.when(i + 2 < n_tiles)
    def _():
        idx_copy(i + 2).start()

    slot = i % 2
    wait_rows(slot)
    w = w_ref[...]
    out = x_ref[...] + w[:, 0:1] * ybuf[slot, 0:tm, :] + w[:, 1:2] * ybuf[slot, tm:2 * tm, :]
    if final_norm:
        out = _rms(out, g_ref[...])
    o_ref[...] = out


def _combine(x1, y, pos, w_assign, g_final, final_norm):
    T, D = x1.shape
    tm = _tile(T, 256)
    n_tiles = T // tm
    kern = functools.partial(_combine_kernel, tm=tm, n_tiles=n_tiles, final_norm=final_norm)
    return pl.pallas_call(
        kern,
        grid=(n_tiles,),
        in_specs=[pl.BlockSpec(memory_space=pl.ANY),
                  pl.BlockSpec(memory_space=pl.ANY),
                  pl.BlockSpec((tm, D), lambda i: (i, 0)),
                  pl.BlockSpec((tm, TOP_K), lambda i: (i, 0)),
                  pl.BlockSpec((1, D), lambda i: (0, 0))],
        out_specs=pl.BlockSpec((tm, D), lambda i: (i, 0)),
        out_shape=jax.ShapeDtypeStruct((T, D), F32),
        scratch_shapes=[pltpu.SMEM((2, TOP_K * tm), jnp.int32),
                        pltpu.VMEM((2, TOP_K * tm, D), F32),
                        pltpu.SemaphoreType.DMA(()),
                        pltpu.SemaphoreType.DMA((2,))],
        compiler_params=_params(("arbitrary",)),
        name="moe_combine",
    )(pos.reshape(-1), y, x1, w_assign, g_final.reshape(1, D))


def _layer(x, B, S, p, g_final, final_norm):
    T, D = x.shape
    W = D // 2
    xn = _norm_bf16(x, p["g_mix"])
    proj = _in_proj(xn, p["w_in"], p["b_gate"], W)
    a = _sgu(proj, p["g_sgu"], p["w_sgu"], p["b_sgu"], W)
    b = _attention(proj, B, S, W)
    merged = _merge(a, b, proj, p["w_proj_a"], p["w_proj_b"], W, D)
    x1 = _out_proj(merged, p["w_out"], x)

    route_tm, (xn2, e_out, p_out, tile_counts) = _route(
        x1, p["g_ffn"], p["w_group"], p["b_group"], p["w_router"], p["b_router"])
    n_exp = p["w_gate"].shape[0]
    blk = 256
    n_blocks = T * TOP_K // blk + n_exp
    pos, row_tok, block_tables = _dispatch_tables(
        e_out[:, :TOP_K], e_out[:, TOP_K:2 * TOP_K], tile_counts[:, 0, :n_exp],
        route_tm, n_exp, blk, n_blocks)
    xs = _dispatch(xn2, row_tok, blk, n_blocks)
    hid = _moe_up(xs, block_tables, p["w_gate"], p["w_up"], blk)
    y = _moe_down(hid, block_tables, p["w_down"], blk)
    return _combine(x1, y, pos, p_out[:, :TOP_K], g_final, final_norm)


def kernel(x, g_mix, w_in, g_sgu, w_sgu, b_sgu, b_gate, w_proj_a, w_proj_b, w_out,
           g_ffn, w_group, b_group, w_router, b_router, w_gate, w_up, w_down, g_final):
    B, S, D = x.shape
    stacked = dict(g_mix=g_mix, w_in=w_in, g_sgu=g_sgu, w_sgu=w_sgu, b_sgu=b_sgu, b_gate=b_gate,
                   w_proj_a=w_proj_a, w_proj_b=w_proj_b, w_out=w_out, g_ffn=g_ffn,
                   w_group=w_group, b_group=b_group, w_router=w_router, b_router=b_router,
                   w_gate=w_gate, w_up=w_up, w_down=w_down)
    depth = w_in.shape[0]
    xf = x.reshape(B * S, D)
    for l in range(depth):
        p = {name: val[l] for name, val in stacked.items()}
        xf = _layer(xf, B, S, p, g_final, final_norm=(l == depth - 1))
    return xf.reshape(B, S, D)
```

```python
import functools

import jax
import jax.numpy as jnp
from jax import lax
from jax.experimental import pallas as pl
from jax.experimental.pallas import tpu as pltpu

F32 = jnp.float32
BF16 = jnp.bfloat16

EPS = 1e-6
CHUNK = 64
SGU_BLOCK = 128
SGU_GROUPS = 8
HEAD_DIM = 128
N_GROUPS_MOE = 4
EXPERTS_PER_GROUP = 8
TOP_K = 2
ROUTE_LANES = 128
USED, FIRST, HAS_NEXT = 1, 2, 4

SKIP_BELOW = 120.0

VMEM_LIMIT = 56 * 2**20


def _tile(n, pref):
    t = min(n, pref)
    assert n % t == 0, (n, pref)
    return t


def _params(sem):
    return pltpu.CompilerParams(dimension_semantics=sem, vmem_limit_bytes=VMEM_LIMIT)


def _sigmoid(x):
    return 0.5 * (1.0 + jnp.tanh(0.5 * x))


def _rms(x, g):
    return x * lax.rsqrt(jnp.mean(x * x, axis=-1, keepdims=True) + EPS) * g


def _norm_kernel(x_ref, g_ref, o_ref):
    o_ref[...] = _rms(x_ref[...], g_ref[...]).astype(o_ref.dtype)


def _norm_bf16(x, g):
    T, D = x.shape
    tm = _tile(T, 512)
    return pl.pallas_call(
        _norm_kernel,
        grid=(T // tm,),
        in_specs=[pl.BlockSpec((tm, D), lambda i: (i, 0)),
                  pl.BlockSpec((1, D), lambda i: (0, 0))],
        out_specs=pl.BlockSpec((tm, D), lambda i: (i, 0)),
        out_shape=jax.ShapeDtypeStruct((T, D), BF16),
        compiler_params=_params(("parallel",)),
        name="norm_bf16",
    )(x, g.reshape(1, D))


def _load_column_tile(w_hbm, stage_ref, wb_ref, sem, j, n_tiles):
    tn = stage_ref.shape[1]

    def copy(t):
        return pltpu.make_async_copy(w_hbm.at[:, pl.ds(pl.multiple_of(t * tn, tn), tn)], stage_ref, sem)

    @pl.when(j == 0)
    def _():
        copy(0).start()

    copy(j).wait()
    wb_ref[...] = stage_ref[...].astype(BF16)

    @pl.when(j + 1 < n_tiles)
    def _():
        copy(j + 1).start()


def _proj_kernel(x_ref, w_hbm, c_ref, o_ref, stage_ref, wb_ref, sem, *, n_gelu, n_plain_end):
    j = pl.program_id(0)

    @pl.when(pl.program_id(1) == 0)
    def _():
        _load_column_tile(w_hbm, stage_ref, wb_ref, sem, j, pl.num_programs(0))

    def project(epilogue):
        cw = min(o_ref.shape[1], 256)
        for c in range(o_ref.shape[1] // cw):
            cols = slice(c * cw, (c + 1) * cw)
            y = jnp.dot(x_ref[...], wb_ref[:, cols], preferred_element_type=F32)
            o_ref[:, cols] = epilogue(y, c_ref[0:1, cols]).astype(o_ref.dtype)

    def gelu_tanh(y, _):
        c1 = 0.7978845608028654
        return (0.5 * y) * (1.0 + jnp.tanh(y * (c1 + (c1 * 0.044715) * (y * y))))

    @pl.when(j < n_gelu)
    def _():
        project(gelu_tanh)

    @pl.when((j >= n_gelu) & (j < n_plain_end))
    def _():
        project(lambda y, scale: y * scale)

    @pl.when(j >= n_plain_end)
    def _():
        project(lambda y, half_b: 0.5 + 0.5 * jnp.tanh(0.5 * y + half_b))


def _proj_epilogue_row(b_gate, W):
    seg = lambda v, n: jnp.full((n,), v, F32)
    row = jnp.concatenate([seg(0.0, 2 * W), seg(HEAD_DIM ** -0.5, W), seg(1.0, 2 * W),
                           0.5 * b_gate.reshape(-1).astype(F32)])
    return jnp.broadcast_to(row[None, :], (8, row.shape[0]))


def _in_proj(xn, w_in, b_gate, W):
    T, D = xn.shape
    N = w_in.shape[1]
    tm = _tile(T, 1024)
    tn = _tile(W, 1024)
    coef = _proj_epilogue_row(b_gate, W)
    assert coef.shape == (8, N)
    kern = functools.partial(_proj_kernel, n_gelu=2 * W // tn, n_plain_end=5 * W // tn)
    return pl.pallas_call(
        kern,
        grid=(N // tn, T // tm),
        in_specs=[pl.BlockSpec((tm, D), lambda j, i: (i, 0)),
                  pl.BlockSpec(memory_space=pl.ANY),
                  pl.BlockSpec((8, tn), lambda j, i: (0, j))],
        out_specs=pl.BlockSpec((tm, tn), lambda j, i: (i, j)),
        out_shape=jax.ShapeDtypeStruct((T, N), BF16),
        scratch_shapes=[pltpu.VMEM((D, tn), F32), pltpu.VMEM((D, tn), BF16),
                        pltpu.SemaphoreType.DMA(())],
        compiler_params=_params(("arbitrary", "arbitrary")),
        name="in_proj",
    )(xn, w_in, coef)


def _sgu_kernel(u_ref, v_ref, g_ref, w_ref, bt_ref, o_ref, *, n_sub, gd):
    v = v_ref[...].astype(F32)
    vn = _rms(v, g_ref[...]).astype(BF16)
    t_chunk = lax.broadcasted_iota(jnp.int32, (SGU_BLOCK, SGU_BLOCK), 0) // CHUNK
    s_chunk = lax.broadcasted_iota(jnp.int32, (SGU_BLOCK, SGU_BLOCK), 1) // CHUNK
    causal = s_chunk <= t_chunk
    for g in range(SGU_GROUPS):
        w = jnp.where(causal, w_ref[g], 0.0).astype(BF16)
        bias = bt_ref[:, g:g + 1]
        cols = slice(g * gd, (g + 1) * gd)
        for n in range(n_sub):
            rows = slice(n * SGU_BLOCK, (n + 1) * SGU_BLOCK)
            mixed = jnp.dot(w, vn[rows, cols], preferred_element_type=F32) + bias
            o_ref[rows, cols] = (u_ref[rows, cols].astype(F32) * mixed).astype(o_ref.dtype)


def _sgu(proj, g_sgu, w_sgu, b_sgu, W):
    T = proj.shape[0]
    tm = _tile(T, 2 * SGU_BLOCK)
    gd = W // SGU_GROUPS
    kern = functools.partial(_sgu_kernel, n_sub=tm // SGU_BLOCK, gd=gd)
    return pl.pallas_call(
        kern,
        grid=(T // tm,),
        in_specs=[pl.BlockSpec((tm, W), lambda i: (i, 0)),
                  pl.BlockSpec((tm, W), lambda i: (i, 1)),
                  pl.BlockSpec((1, W), lambda i: (0, 0)),
                  pl.BlockSpec((SGU_GROUPS, SGU_BLOCK, SGU_BLOCK), lambda i: (0, 0, 0)),
                  pl.BlockSpec((SGU_BLOCK, SGU_GROUPS), lambda i: (0, 0))],
        out_specs=pl.BlockSpec((tm, W), lambda i: (i, 0)),
        out_shape=jax.ShapeDtypeStruct((T, W), BF16),
        compiler_params=_params(("parallel",)),
        name="sgu",
    )(proj, proj, g_sgu.reshape(1, W), w_sgu, b_sgu.T)


def _attn_kernel(q_ref, k_ref, v_ref, o_ref, acc_ref, carry_ref, *, tq, heads):
    qi = pl.program_id(2)
    jj = lax.broadcasted_iota(jnp.int32, (tq, tq), 0)
    ss = lax.broadcasted_iota(jnp.int32, (tq, tq), 1)
    tri = (jj >= ss).astype(BF16)
    before = ss < jj

    def step(kb, diagonal):
        rows = pl.ds(pl.multiple_of(kb * tq, tq), tq)
        hcols = [slice(h * HEAD_DIM, (h + 1) * HEAD_DIM) for h in range(heads)]
        zs = [lax.dot_general(q_ref[:, c], k_ref[rows, c], (((1,), (1,)), ((), ())),
                              preferred_element_type=F32) for c in hcols]
        logs = []
        for z in zs:
            lg = -(jnp.maximum(z, 0.0) + jnp.log(1.0 + jnp.exp(-jnp.abs(z))))
            if diagonal:
                lg = jnp.where(before, lg, 0.0)
            logs.append(lg.astype(BF16))
        incls = [jnp.dot(lg, tri, preferred_element_type=F32) for lg in logs]
        weights, worst = [], None
        for h, (z, incl) in enumerate(zip(zs, incls)):
            if diagonal:
                a = jnp.where(before, jnp.exp(z + incl), 0.0)
                carry = incl[:, 0:1]
            else:
                a = jnp.exp(z + incl + carry_ref[h])
                carry = carry_ref[h] + incl[:, 0:1]
            carry_ref[h] = carry
            weights.append(a.astype(BF16))
            worst = carry if worst is None else jnp.maximum(worst, carry)
        for c, a in zip(hcols, weights):
            pv = jnp.dot(a, v_ref[rows, c], preferred_element_type=F32)
            if diagonal:
                acc_ref[:, c] = pv
            else:
                acc_ref[:, c] += pv
        return jnp.max(worst)

    def cond(state):
        kb, worst = state
        return (kb >= 0) & (worst > -SKIP_BELOW)

    def body(state):
        kb, _ = state
        return kb - 1, step(kb, False)

    lax.while_loop(cond, body, (qi - 1, step(qi, True)))
    o_ref[...] = acc_ref[...].astype(o_ref.dtype)


def _attention(proj, B, S, W):
    T = proj.shape[0]
    n_heads = W // HEAD_DIM
    heads = min(4, n_heads)
    assert n_heads % heads == 0
    hw = heads * HEAD_DIM
    tq = _tile(S, 256)
    nq = S // tq
    q0, k0, v0 = 2 * W // hw, 3 * W // hw, 4 * W // hw
    kern = functools.partial(_attn_kernel, tq=tq, heads=heads)
    return pl.pallas_call(
        kern,
        grid=(B, n_heads // heads, nq),
        in_specs=[pl.BlockSpec((tq, hw), lambda b, g, i: (b * nq + i, q0 + g)),
                  pl.BlockSpec((S, hw), lambda b, g, i: (b, k0 + g)),
                  pl.BlockSpec((S, hw), lambda b, g, i: (b, v0 + g))],
        out_specs=pl.BlockSpec((tq, hw), lambda b, g, i: (b * nq + i, g)),
        out_shape=jax.ShapeDtypeStruct((T, W), BF16),
        scratch_shapes=[pltpu.VMEM((tq, hw), F32),
                        pltpu.VMEM((heads, tq, 1), F32)],
        compiler_params=_params(("parallel", "parallel", "arbitrary")),
        name="stick_breaking",
    )(proj, proj, proj)


def _merge_kernel(a_ref, b_ref, pa_hbm, pb_hbm, ga_ref, gb_ref, o_ref,
                  sa_ref, sb_ref, pab_ref, pbb_ref, sem):
    @pl.when(pl.program_id(1) == 0)
    def _():
        j, n = pl.program_id(0), pl.num_programs(0)
        _load_column_tile(pa_hbm, sa_ref, pab_ref, sem.at[0], j, n)
        _load_column_tile(pb_hbm, sb_ref, pbb_ref, sem.at[1], j, n)

    cw = min(o_ref.shape[1], 256)
    for c in range(o_ref.shape[1] // cw):
        cols = slice(c * cw, (c + 1) * cw)
        ya = jnp.dot(a_ref[...], pab_ref[:, cols], preferred_element_type=F32)
        yb = jnp.dot(b_ref[...], pbb_ref[:, cols], preferred_element_type=F32)
        o_ref[:, cols] = (ga_ref[:, cols].astype(F32) * ya
                          + gb_ref[:, cols].astype(F32) * yb).astype(o_ref.dtype)


def _merge(a, b, proj, w_pa, w_pb, W, D):
    T = a.shape[0]
    tm = _tile(T, 512)
    tn = _tile(W, 1024)
    ga0 = 5 * W // tn
    gb0 = (5 * W + D) // tn
    return pl.pallas_call(
        _merge_kernel,
        grid=(D // tn, T // tm),
        in_specs=[pl.BlockSpec((tm, W), lambda j, i: (i, 0)),
                  pl.BlockSpec((tm, W), lambda j, i: (i, 0)),
                  pl.BlockSpec(memory_space=pl.ANY),
                  pl.BlockSpec(memory_space=pl.ANY),
                  pl.BlockSpec((tm, tn), lambda j, i: (i, ga0 + j)),
                  pl.BlockSpec((tm, tn), lambda j, i: (i, gb0 + j))],
        out_specs=pl.BlockSpec((tm, tn), lambda j, i: (i, j)),
        out_shape=jax.ShapeDtypeStruct((T, D), BF16),
        scratch_shapes=[pltpu.VMEM((W, tn), F32), pltpu.VMEM((W, tn), F32),
                        pltpu.VMEM((W, tn), BF16), pltpu.VMEM((W, tn), BF16),
                        pltpu.SemaphoreType.DMA((2,))],
        compiler_params=_params(("arbitrary", "arbitrary")),
        name="merge",
    )(a, b, w_pa, w_pb, proj, proj)


def _out_kernel(m_ref, w_hbm, x_ref, o_ref, stage_ref, wb_ref, sem):
    @pl.when(pl.program_id(1) == 0)
    def _():
        _load_column_tile(w_hbm, stage_ref, wb_ref, sem, pl.program_id(0), pl.num_programs(0))

    cw = min(o_ref.shape[1], 256)
    for c in range(o_ref.shape[1] // cw):
        cols = slice(c * cw, (c + 1) * cw)
        o_ref[:, cols] = x_ref[:, cols] + jnp.dot(m_ref[...], wb_ref[:, cols],
                                                  preferred_element_type=F32)


def _out_proj(merged, w_out, x):
    T, D = x.shape
    tm = _tile(T, 512)
    tn = _tile(D, 1024)
    return pl.pallas_call(
        _out_kernel,
        grid=(D // tn, T // tm),
        in_specs=[pl.BlockSpec((tm, D), lambda j, i: (i, 0)),
                  pl.BlockSpec(memory_space=pl.ANY),
                  pl.BlockSpec((tm, tn), lambda j, i: (i, j))],
        out_specs=pl.BlockSpec((tm, tn), lambda j, i: (i, j)),
        out_shape=jax.ShapeDtypeStruct((T, D), F32),
        scratch_shapes=[pltpu.VMEM((D, tn), F32), pltpu.VMEM((D, tn), BF16),
                        pltpu.SemaphoreType.DMA(())],
        compiler_params=_params(("arbitrary", "arbitrary")),
        name="out_proj",
    )(merged, w_out, x)


def _route_kernel(x_ref, g_ref, wr_ref, br_ref, xn_ref, e_ref, p_ref, c_ref):
    xn = _rms(x_ref[...], g_ref[...])
    xn_ref[...] = xn
    wr = wr_ref[...]
    wr_hi = wr.astype(BF16)
    wr_lo = (wr - wr_hi.astype(F32)).astype(BF16)
    xn_hi = xn.astype(BF16)
    xn_lo = (xn - xn_hi.astype(F32)).astype(BF16)
    logits = (jnp.dot(xn_hi, wr_hi, preferred_element_type=F32)
              + jnp.dot(xn_lo, wr_hi, preferred_element_type=F32)
              + jnp.dot(xn_hi, wr_lo, preferred_element_type=F32)) + br_ref[...]
    tm = logits.shape[0]
    lane = lax.broadcasted_iota(jnp.int32, logits.shape, 1)
    neg = -jnp.inf

    def top(vals):
        m = jnp.max(vals, axis=1, keepdims=True)
        idx = jnp.min(jnp.where(vals == m, lane, ROUTE_LANES), axis=1, keepdims=True)
        return m, idx

    is_group = lane < N_GROUPS_MOE
    gmax, grp = top(jnp.where(is_group, logits, neg))
    p_grp = 1.0 / jnp.sum(jnp.where(is_group, jnp.exp(logits - gmax), 0.0), axis=1, keepdims=True)
    first = N_GROUPS_MOE + grp * EXPERTS_PER_GROUP
    local = jnp.where((lane >= first) & (lane < first + EXPERTS_PER_GROUP), logits, neg)
    m1, i1 = top(local)
    m2, i2 = top(jnp.where(lane == i1, neg, local))
    e2 = jnp.exp(m2 - m1)
    w1 = p_grp / (1.0 + e2)
    w2 = p_grp * e2 / (1.0 + e2)
    ex1 = i1 - N_GROUPS_MOE
    ex2 = i2 - N_GROUPS_MOE

    hot1 = lane == ex1
    hot2 = lane == ex2
    chosen = (hot1 | hot2).astype(BF16)
    row = lax.broadcasted_iota(jnp.int32, (tm, tm), 0)
    col = lax.broadcasted_iota(jnp.int32, (tm, tm), 1)
    earlier = jnp.dot((col < row).astype(BF16), chosen, preferred_element_type=F32)
    r1 = jnp.sum(jnp.where(hot1, earlier, 0.0), axis=1, keepdims=True).astype(jnp.int32)
    r2 = jnp.sum(jnp.where(hot2, earlier, 0.0), axis=1, keepdims=True).astype(jnp.int32)
    counts = jnp.sum(chosen.astype(F32), axis=0, keepdims=True).astype(jnp.int32)

    e_ref[...] = jnp.where(lane == 0, ex1, jnp.where(lane == 1, ex2,
                           jnp.where(lane == 2, r1, jnp.where(lane == 3, r2, 0))))
    p_ref[...] = jnp.where(lane == 0, w1, jnp.where(lane == 1, w2, 0.0))
    c_ref[...] = jnp.broadcast_to(counts[None], c_ref.shape)


def _route(x1, g_ffn, w_group, b_group, w_router, b_router):
    T, D = x1.shape
    n_logit = w_group.shape[1] + w_router.shape[1]
    assert n_logit <= ROUTE_LANES
    wr = jnp.zeros((D, ROUTE_LANES), F32).at[:, :n_logit].set(jnp.concatenate([w_group, w_router], axis=1))
    br = jnp.zeros((1, ROUTE_LANES), F32).at[0, :n_logit].set(jnp.concatenate([b_group, b_router]))
    tm = _tile(T, 256)
    n_tiles = T // tm
    return tm, pl.pallas_call(
        _route_kernel,
        grid=(n_tiles,),
        in_specs=[pl.BlockSpec((tm, D), lambda i: (i, 0)),
                  pl.BlockSpec((1, D), lambda i: (0, 0)),
                  pl.BlockSpec((D, ROUTE_LANES), lambda i: (0, 0)),
                  pl.BlockSpec((1, ROUTE_LANES), lambda i: (0, 0))],
        out_specs=[pl.BlockSpec((tm, D), lambda i: (i, 0)),
                   pl.BlockSpec((tm, ROUTE_LANES), lambda i: (i, 0)),
                   pl.BlockSpec((tm, ROUTE_LANES), lambda i: (i, 0)),
                   pl.BlockSpec((1, 8, ROUTE_LANES), lambda i: (i, 0, 0))],
        out_shape=[jax.ShapeDtypeStruct((T, D), F32),
                   jax.ShapeDtypeStruct((T, ROUTE_LANES), jnp.int32),
                   jax.ShapeDtypeStruct((T, ROUTE_LANES), F32),
                   jax.ShapeDtypeStruct((n_tiles, 8, ROUTE_LANES), jnp.int32)],
        compiler_params=_params(("parallel",)),
        name="route",
    )(x1, g_ffn.reshape(1, D), wr, br)


def _dispatch_tables(experts, ranks, tile_counts, route_tm, n_exp, blk, n_blocks):
    T = experts.shape[0]
    i32 = jnp.int32
    tile_base = jnp.cumsum(tile_counts, axis=0) - tile_counts
    counts = jnp.sum(tile_counts, axis=0)
    padded = (counts + blk - 1) // blk * blk
    pad_end = jnp.cumsum(padded)
    pad_start = pad_end - padded
    base = jnp.repeat(tile_base + pad_start[None, :], route_tm, axis=0)
    hot = experts[:, :, None] == jnp.arange(n_exp, dtype=i32)[None, None, :]
    pos = (jnp.sum(jnp.where(hot, base[:, None, :], 0), axis=2) + ranks).astype(i32)
    token = jnp.broadcast_to(jnp.arange(T, dtype=i32)[:, None], pos.shape)
    pad_tok = jnp.arange(n_blocks * blk, dtype=i32) % T
    row_tok = pad_tok.at[pos.reshape(-1)].set(token.reshape(-1), unique_indices=True)

    n_used = pad_end[-1] // blk
    bstart = jnp.arange(n_blocks, dtype=i32) * blk
    be = jnp.minimum(jnp.sum((pad_end[None, :] <= bstart[:, None]).astype(i32), axis=1), n_exp - 1)
    be = jnp.where(jnp.arange(n_blocks) < n_used, be, be[n_used - 1])

    blk_id = jnp.arange(n_blocks, dtype=i32)
    blk_used = blk_id < n_used
    blk_first = blk_used & (blk_id == pad_start[be] // blk)
    nxt_blk = blk_id + jnp.maximum(padded[be] // blk, 1)
    blk_has_next = blk_first & (nxt_blk < n_used)
    nxt_blk = jnp.minimum(nxt_blk, n_blocks - 1)
    block_flags = (USED * blk_used + FIRST * blk_first + HAS_NEXT * blk_has_next).astype(i32)
    return pos, row_tok, (be.astype(i32), be[nxt_blk].astype(i32), block_flags)


def _dispatch_kernel(flag_ref, tok_hbm, x_hbm, o_ref, idx_smem, xbuf, idx_sem, row_sem, *, blk, n_blocks):
    i = pl.program_id(0)

    def used(b):
        return (flag_ref[jnp.minimum(b, n_blocks - 1)] & USED) != 0

    def idx_copy(b):
        return pltpu.make_async_copy(tok_hbm.at[pl.ds(pl.multiple_of(b * blk, blk), blk)],
                                     idx_smem.at[b % 2], idx_sem)

    def row_copy(tok, slot, r):
        return pltpu.make_async_copy(x_hbm.at[pl.ds(tok, 1), :], xbuf.at[slot, pl.ds(r, 1), :],
                                     row_sem.at[slot])

    def issue_rows(b):
        slot = b % 2

        def body(r, c):
            row_copy(idx_smem[slot, r], slot, r).start()
            return c
        lax.fori_loop(0, blk, body, 0, unroll=16)

    def wait_rows(slot):
        pltpu.make_async_copy(x_hbm.at[pl.ds(0, blk), :], xbuf.at[slot], row_sem.at[slot]).wait()

    @pl.when(i == 0)
    def _():
        idx_copy(0).start()
        idx_copy(0).wait()
        issue_rows(0)
        if n_blocks > 1:
            idx_copy(1).start()

    @pl.when(i + 1 < n_blocks)
    def _():
        idx_copy(i + 1).wait()

        @pl.when(used(i + 1))
        def _():
            issue_rows(i + 1)

    @pl.when(i + 2 < n_blocks)
    def _():
        idx_copy(i + 2).start()

    slot = i % 2

    @pl.when(used(i))
    def _():
        wait_rows(slot)
        o_ref[...] = xbuf[slot].astype(o_ref.dtype)

    @pl.when(jnp.logical_not(used(i)))
    def _():
        o_ref[...] = jnp.zeros_like(o_ref)


def _dispatch(xn, row_tok, block_flags, blk, n_blocks):
    T, D = xn.shape
    kern = functools.partial(_dispatch_kernel, blk=blk, n_blocks=n_blocks)
    grid_spec = pltpu.PrefetchScalarGridSpec(
        num_scalar_prefetch=1,
        grid=(n_blocks,),
        in_specs=[pl.BlockSpec(memory_space=pl.ANY),
                  pl.BlockSpec(memory_space=pl.ANY)],
        out_specs=pl.BlockSpec((blk, D), lambda i, f: (i, 0)),
        scratch_shapes=[pltpu.SMEM((2, blk), jnp.int32),
                        pltpu.VMEM((2, blk, D), F32),
                        pltpu.SemaphoreType.DMA(()),
                        pltpu.SemaphoreType.DMA((2,))],
    )
    return pl.pallas_call(
        kern,
        grid_spec=grid_spec,
        out_shape=jax.ShapeDtypeStruct((n_blocks * blk, D), BF16),
        compiler_params=_params(("arbitrary",)),
        name="moe_dispatch",
    )(block_flags, row_tok, xn)


def _moe_up_kernel(e_ref, next_e_ref, flag_ref, x_ref, wg_hbm, wu_hbm, o_ref,
                   sg_ref, su_ref, wgb_ref, wub_ref, sem):
    i = pl.program_id(0)
    flag = flag_ref[i]
    cw = min(o_ref.shape[1], 256)

    def copies(e):
        return (pltpu.make_async_copy(wg_hbm.at[e], sg_ref, sem.at[0]),
                pltpu.make_async_copy(wu_hbm.at[e], su_ref, sem.at[1]))

    def project(convert):
        for c in range(o_ref.shape[1] // cw):
            cols = slice(c * cw, (c + 1) * cw)
            if convert:
                wgb_ref[:, cols] = sg_ref[:, cols].astype(BF16)
                wub_ref[:, cols] = su_ref[:, cols].astype(BF16)
            g = jnp.dot(x_ref[...], wgb_ref[:, cols], preferred_element_type=F32)
            u = jnp.dot(x_ref[...], wub_ref[:, cols], preferred_element_type=F32)
            o_ref[:, cols] = (g * _sigmoid(g) * u).astype(o_ref.dtype)

    @pl.when((flag & FIRST) != 0)
    def _():
        @pl.when(i == 0)
        def _():
            for cp in copies(e_ref[0]):
                cp.start()

        for cp in copies(e_ref[i]):
            cp.wait()
        project(True)

        @pl.when((flag & HAS_NEXT) != 0)
        def _():
            for cp in copies(next_e_ref[i]):
                cp.start()

    @pl.when((flag & (USED | FIRST)) == USED)
    def _():
        project(False)

    @pl.when((flag & USED) == 0)
    def _():
        o_ref[...] = jnp.zeros_like(o_ref)


def _moe_up(xs, block_tables, w_gate, w_up, blk):
    n_rows, D = xs.shape
    hid = w_gate.shape[2]
    grid_spec = pltpu.PrefetchScalarGridSpec(
        num_scalar_prefetch=3,
        grid=(n_rows // blk,),
        in_specs=[pl.BlockSpec((blk, D), lambda i, e, ne, f: (i, 0)),
                  pl.BlockSpec(memory_space=pl.ANY),
                  pl.BlockSpec(memory_space=pl.ANY)],
        out_specs=pl.BlockSpec((blk, hid), lambda i, e, ne, f: (i, 0)),
        scratch_shapes=[pltpu.VMEM((D, hid), F32), pltpu.VMEM((D, hid), F32),
                        pltpu.VMEM((D, hid), BF16), pltpu.VMEM((D, hid), BF16),
                        pltpu.SemaphoreType.DMA((2,))],
    )
    return pl.pallas_call(
        _moe_up_kernel,
        grid_spec=grid_spec,
        out_shape=jax.ShapeDtypeStruct((n_rows, hid), BF16),
        compiler_params=_params(("arbitrary",)),
        name="moe_up",
    )(*block_tables, xs, w_gate, w_up)


def _moe_down_kernel(e_ref, next_e_ref, flag_ref, h_ref, wd_hbm, o_ref, stage_ref, wdb_ref, sem):
    i = pl.program_id(0)
    flag = flag_ref[i]
    cw = min(o_ref.shape[1], 256)

    def copy(e):
        return pltpu.make_async_copy(wd_hbm.at[e], stage_ref, sem)

    def project(convert):
        for c in range(o_ref.shape[1] // cw):
            cols = slice(c * cw, (c + 1) * cw)
            if convert:
                wdb_ref[:, cols] = stage_ref[:, cols].astype(BF16)
            o_ref[:, cols] = jnp.dot(h_ref[...], wdb_ref[:, cols], preferred_element_type=F32)

    @pl.when((flag & FIRST) != 0)
    def _():
        @pl.when(i == 0)
        def _():
            copy(e_ref[0]).start()

        copy(e_ref[i]).wait()
        project(True)

        @pl.when((flag & HAS_NEXT) != 0)
        def _():
            copy(next_e_ref[i]).start()

    @pl.when((flag & (USED | FIRST)) == USED)
    def _():
        project(False)

    @pl.when((flag & USED) == 0)
    def _():
        o_ref[...] = jnp.zeros_like(o_ref)


def _moe_down(hid, block_tables, w_down, blk):
    n_rows, H = hid.shape
    D = w_down.shape[2]
    grid_spec = pltpu.PrefetchScalarGridSpec(
        num_scalar_prefetch=3,
        grid=(n_rows // blk,),
        in_specs=[pl.BlockSpec((blk, H), lambda i, e, ne, f: (i, 0)),
                  pl.BlockSpec(memory_space=pl.ANY)],
        out_specs=pl.BlockSpec((blk, D), lambda i, e, ne, f: (i, 0)),
        scratch_shapes=[pltpu.VMEM((H, D), F32), pltpu.VMEM((H, D), BF16),
                        pltpu.SemaphoreType.DMA(())],
    )
    return pl.pallas_call(
        _moe_down_kernel,
        grid_spec=grid_spec,
        out_shape=jax.ShapeDtypeStruct((n_rows, D), F32),
        compiler_params=_params(("arbitrary",)),
        name="moe_down",
    )(*block_tables, hid, w_down)


def _combine_kernel(pos_hbm, y_hbm, x_ref, w_ref, g_ref, o_ref,
                    idx_smem, ybuf, idx_sem, row_sem, *, tm, n_tiles, final_norm):
    i = pl.program_id(0)
    n_idx = TOP_K * tm

    def idx_copy(t):
        return pltpu.make_async_copy(pos_hbm.at[pl.ds(pl.multiple_of(t * n_idx, n_idx), n_idx)],
                                     idx_smem.at[t % 2], idx_sem)

    def row_copy(src, slot, dst):
        return pltpu.make_async_copy(y_hbm.at[pl.ds(src, 1), :], ybuf.at[slot, pl.ds(dst, 1), :],
                                     row_sem.at[slot])

    def issue_rows(t):
        slot = t % 2

        def body(r, c):
            for k in range(TOP_K):
                row_copy(idx_smem[slot, TOP_K * r + k], slot, k * tm + r).start()
            return c
        lax.fori_loop(0, tm, body, 0, unroll=8)

    def wait_rows(slot):
        pltpu.make_async_copy(y_hbm.at[pl.ds(0, n_idx), :], ybuf.at[slot], row_sem.at[slot]).wait()

    @pl.when(i == 0)
    def _():
        idx_copy(0).start()
        idx_copy(0).wait()
        issue_rows(0)
        if n_tiles > 1:
            idx_copy(1).start()

    @pl.when(i + 1 < n_tiles)
    def _():
        idx_copy(i + 1).wait()
        issue_rows(i + 1)

    @pl.when(i + 2 < n_tiles)
    def _():
        idx_copy(i + 2).start()

    slot = i % 2
    wait_rows(slot)
    w = w_ref[...]
    out = x_ref[...] + w[:, 0:1] * ybuf[slot, 0:tm, :] + w[:, 1:2] * ybuf[slot, tm:2 * tm, :]
    if final_norm:
        out = _rms(out, g_ref[...])
    o_ref[...] = out


def _combine(x1, y, pos, w_assign, g_final, final_norm):
    T, D = x1.shape
    tm = _tile(T, 256)
    n_tiles = T // tm
    kern = functools.partial(_combine_kernel, tm=tm, n_tiles=n_tiles, final_norm=final_norm)
    return pl.pallas_call(
        kern,
        grid=(n_tiles,),
        in_specs=[pl.BlockSpec(memory_space=pl.ANY),
                  pl.BlockSpec(memory_space=pl.ANY),
                  pl.BlockSpec((tm, D), lambda i: (i, 0)),
                  pl.BlockSpec((tm, TOP_K), lambda i: (i, 0)),
                  pl.BlockSpec((1, D), lambda i: (0, 0))],
        out_specs=pl.BlockSpec((tm, D), lambda i: (i, 0)),
        out_shape=jax.ShapeDtypeStruct((T, D), F32),
        scratch_shapes=[pltpu.SMEM((2, TOP_K * tm), jnp.int32),
                        pltpu.VMEM((2, TOP_K * tm, D), F32),
                        pltpu.SemaphoreType.DMA(()),
                        pltpu.SemaphoreType.DMA((2,))],
        compiler_params=_params(("arbitrary",)),
        name="moe_combine",
    )(pos.reshape(-1), y, x1, w_assign, g_final.reshape(1, D))


def _layer(x, B, S, p, g_final, final_norm):
    T, D = x.shape
    W = D // 2
    xn = _norm_bf16(x, p["g_mix"])
    proj = _in_proj(xn, p["w_in"], p["b_gate"], W)
    a = _sgu(proj, p["g_sgu"], p["w_sgu"], p["b_sgu"], W)
    b = _attention(proj, B, S, W)
    merged = _merge(a, b, proj, p["w_proj_a"], p["w_proj_b"], W, D)
    x1 = _out_proj(merged, p["w_out"], x)

    route_tm, (xn2, e_out, p_out, tile_counts) = _route(
        x1, p["g_ffn"], p["w_group"], p["b_group"], p["w_router"], p["b_router"])
    n_exp = p["w_gate"].shape[0]
    blk = 256
    n_blocks = T * TOP_K // blk + n_exp
    pos, row_tok, block_tables = _dispatch_tables(
        e_out[:, :TOP_K], e_out[:, TOP_K:2 * TOP_K], tile_counts[:, 0, :n_exp],
        route_tm, n_exp, blk, n_blocks)
    xs = _dispatch(xn2, row_tok, block_tables[2], blk, n_blocks)
    hid = _moe_up(xs, block_tables, p["w_gate"], p["w_up"], blk)
    y = _moe_down(hid, block_tables, p["w_down"], blk)
    return _combine(x1, y, pos, p_out[:, :TOP_K], g_final, final_norm)


def kernel(x, g_mix, w_in, g_sgu, w_sgu, b_sgu, b_gate, w_proj_a, w_proj_b, w_out,
           g_ffn, w_group, b_group, w_router, b_router, w_gate, w_up, w_down, g_final):
    B, S, D = x.shape
    stacked = dict(g_mix=g_mix, w_in=w_in, g_sgu=g_sgu, w_sgu=w_sgu, b_sgu=b_sgu, b_gate=b_gate,
                   w_proj_a=w_proj_a, w_proj_b=w_proj_b, w_out=w_out, g_ffn=g_ffn,
                   w_group=w_group, b_group=b_group, w_router=w_router, b_router=b_router,
                   w_gate=w_gate, w_up=w_up, w_down=w_down)
    depth = w_in.shape[0]
    xf = x.reshape(B * S, D)
    for l in range(depth):
        p = {name: val[l] for name, val in stacked.items()}
        xf = _layer(xf, B, S, p, g_final, final_norm=(l == depth - 1))
    return xf.reshape(B, S, D)
```

```python
import functools

import jax
import jax.numpy as jnp
from jax import lax
from jax.experimental import pallas as pl
from jax.experimental.pallas import tpu as pltpu

F32 = jnp.float32
BF16 = jnp.bfloat16

EPS = 1e-6
CHUNK = 64
SGU_BLOCK = 128
SGU_GROUPS = 8
HEAD_DIM = 128
N_GROUPS_MOE = 4
EXPERTS_PER_GROUP = 8
TOP_K = 2
ROUTE_LANES = 128
USED, FIRST, HAS_NEXT = 1, 2, 4

SKIP_BELOW = 120.0

VMEM_LIMIT = 56 * 2**20


def _tile(n, pref):
    t = min(n, pref)
    assert n % t == 0, (n, pref)
    return t


def _params(sem):
    return pltpu.CompilerParams(dimension_semantics=sem, vmem_limit_bytes=VMEM_LIMIT)


def _sigmoid(x):
    return 0.5 * (1.0 + jnp.tanh(0.5 * x))


def _rms(x, g):
    return x * lax.rsqrt(jnp.mean(x * x, axis=-1, keepdims=True) + EPS) * g


def _norm_kernel(x_ref, g_ref, o_ref):
    o_ref[...] = _rms(x_ref[...], g_ref[...]).astype(o_ref.dtype)


def _norm_bf16(x, g):
    T, D = x.shape
    tm = _tile(T, 512)
    return pl.pallas_call(
        _norm_kernel,
        grid=(T // tm,),
        in_specs=[pl.BlockSpec((tm, D), lambda i: (i, 0)),
                  pl.BlockSpec((1, D), lambda i: (0, 0))],
        out_specs=pl.BlockSpec((tm, D), lambda i: (i, 0)),
        out_shape=jax.ShapeDtypeStruct((T, D), BF16),
        compiler_params=_params(("parallel",)),
        name="norm_bf16",
    )(x, g.reshape(1, D))


def _load_column_tile(w_hbm, stage_ref, wb_ref, sem, j, n_tiles):
    tn = stage_ref.shape[1]

    def copy(t):
        return pltpu.make_async_copy(w_hbm.at[:, pl.ds(pl.multiple_of(t * tn, tn), tn)], stage_ref, sem)

    @pl.when(j == 0)
    def _():
        copy(0).start()

    copy(j).wait()
    wb_ref[...] = stage_ref[...].astype(BF16)

    @pl.when(j + 1 < n_tiles)
    def _():
        copy(j + 1).start()


def _proj_kernel(x_ref, w_hbm, c_ref, o_ref, stage_ref, wb_ref, sem, *, n_gelu, n_plain_end):
    j = pl.program_id(0)

    @pl.when(pl.program_id(1) == 0)
    def _():
        _load_column_tile(w_hbm, stage_ref, wb_ref, sem, j, pl.num_programs(0))

    def project(epilogue):
        cw = min(o_ref.shape[1], 256)
        for c in range(o_ref.shape[1] // cw):
            cols = slice(c * cw, (c + 1) * cw)
            y = jnp.dot(x_ref[...], wb_ref[:, cols], preferred_element_type=F32)
            o_ref[:, cols] = epilogue(y, c_ref[0:1, cols]).astype(o_ref.dtype)

    def gelu_tanh(y, _):
        c1 = 0.7978845608028654
        return (0.5 * y) * (1.0 + jnp.tanh(y * (c1 + (c1 * 0.044715) * (y * y))))

    @pl.when(j < n_gelu)
    def _():
        project(gelu_tanh)

    @pl.when((j >= n_gelu) & (j < n_plain_end))
    def _():
        project(lambda y, scale: y * scale)

    @pl.when(j >= n_plain_end)
    def _():
        project(lambda y, half_b: 0.5 + 0.5 * jnp.tanh(0.5 * y + half_b))


def _proj_epilogue_row(b_gate, W):
    seg = lambda v, n: jnp.full((n,), v, F32)
    row = jnp.concatenate([seg(0.0, 2 * W), seg(HEAD_DIM ** -0.5, W), seg(1.0, 2 * W),
                           0.5 * b_gate.reshape(-1).astype(F32)])
    return jnp.broadcast_to(row[None, :], (8, row.shape[0]))


def _in_proj(xn, w_in, b_gate, W):
    T, D = xn.shape
    N = w_in.shape[1]
    tm = _tile(T, 1024)
    tn = _tile(W, 1024)
    coef = _proj_epilogue_row(b_gate, W)
    assert coef.shape == (8, N)
    kern = functools.partial(_proj_kernel, n_gelu=2 * W // tn, n_plain_end=5 * W // tn)
    return pl.pallas_call(
        kern,
        grid=(N // tn, T // tm),
        in_specs=[pl.BlockSpec((tm, D), lambda j, i: (i, 0)),
                  pl.BlockSpec(memory_space=pl.ANY),
                  pl.BlockSpec((8, tn), lambda j, i: (0, j))],
        out_specs=pl.BlockSpec((tm, tn), lambda j, i: (i, j)),
        out_shape=jax.ShapeDtypeStruct((T, N), BF16),
        scratch_shapes=[pltpu.VMEM((D, tn), F32), pltpu.VMEM((D, tn), BF16),
                        pltpu.SemaphoreType.DMA(())],
        compiler_params=_params(("arbitrary", "arbitrary")),
        name="in_proj",
    )(xn, w_in, coef)


def _sgu_kernel(u_ref, v_ref, g_ref, w_ref, bt_ref, o_ref, *, n_sub, gd):
    v = v_ref[...].astype(F32)
    vn = _rms(v, g_ref[...]).astype(BF16)
    t_chunk = lax.broadcasted_iota(jnp.int32, (SGU_BLOCK, SGU_BLOCK), 0) // CHUNK
    s_chunk = lax.broadcasted_iota(jnp.int32, (SGU_BLOCK, SGU_BLOCK), 1) // CHUNK
    causal = s_chunk <= t_chunk
    for g in range(SGU_GROUPS):
        w = jnp.where(causal, w_ref[g], 0.0).astype(BF16)
        bias = bt_ref[:, g:g + 1]
        cols = slice(g * gd, (g + 1) * gd)
        for n in range(n_sub):
            rows = slice(n * SGU_BLOCK, (n + 1) * SGU_BLOCK)
            mixed = jnp.dot(w, vn[rows, cols], preferred_element_type=F32) + bias
            o_ref[rows, cols] = (u_ref[rows, cols].astype(F32) * mixed).astype(o_ref.dtype)


def _sgu(proj, g_sgu, w_sgu, b_sgu, W):
    T = proj.shape[0]
    tm = _tile(T, 2 * SGU_BLOCK)
    gd = W // SGU_GROUPS
    kern = functools.partial(_sgu_kernel, n_sub=tm // SGU_BLOCK, gd=gd)
    return pl.pallas_call(
        kern,
        grid=(T // tm,),
        in_specs=[pl.BlockSpec((tm, W), lambda i: (i, 0)),
                  pl.BlockSpec((tm, W), lambda i: (i, 1)),
                  pl.BlockSpec((1, W), lambda i: (0, 0)),
                  pl.BlockSpec((SGU_GROUPS, SGU_BLOCK, SGU_BLOCK), lambda i: (0, 0, 0)),
                  pl.BlockSpec((SGU_BLOCK, SGU_GROUPS), lambda i: (0, 0))],
        out_specs=pl.BlockSpec((tm, W), lambda i: (i, 0)),
        out_shape=jax.ShapeDtypeStruct((T, W), BF16),
        compiler_params=_params(("parallel",)),
        name="sgu",
    )(proj, proj, g_sgu.reshape(1, W), w_sgu, b_sgu.T)


def _attn_kernel(q_ref, k_ref, v_ref, o_ref, acc_ref, carry_ref, *, tq, heads):
    qi = pl.program_id(2)
    jj = lax.broadcasted_iota(jnp.int32, (tq, tq), 0)
    ss = lax.broadcasted_iota(jnp.int32, (tq, tq), 1)
    tri = (jj >= ss).astype(BF16)
    before = ss < jj

    def step(kb, diagonal):
        rows = pl.ds(pl.multiple_of(kb * tq, tq), tq)
        hcols = [slice(h * HEAD_DIM, (h + 1) * HEAD_DIM) for h in range(heads)]
        zs = [lax.dot_general(q_ref[:, c], k_ref[rows, c], (((1,), (1,)), ((), ())),
                              preferred_element_type=F32) for c in hcols]
        logs = []
        for z in zs:
            lg = -(jnp.maximum(z, 0.0) + jnp.log(1.0 + jnp.exp(-jnp.abs(z))))
            if diagonal:
                lg = jnp.where(before, lg, 0.0)
            logs.append(lg.astype(BF16))
        incls = [jnp.dot(lg, tri, preferred_element_type=F32) for lg in logs]
        weights, worst = [], None
        for h, (z, incl) in enumerate(zip(zs, incls)):
            if diagonal:
                a = jnp.where(before, jnp.exp(z + incl), 0.0)
                carry = incl[:, 0:1]
            else:
                a = jnp.exp(z + incl + carry_ref[h])
                carry = carry_ref[h] + incl[:, 0:1]
            carry_ref[h] = carry
            weights.append(a.astype(BF16))
            worst = carry if worst is None else jnp.maximum(worst, carry)
        for c, a in zip(hcols, weights):
            pv = jnp.dot(a, v_ref[rows, c], preferred_element_type=F32)
            if diagonal:
                acc_ref[:, c] = pv
            else:
                acc_ref[:, c] += pv
        return jnp.max(worst)

    def cond(state):
        kb, worst = state
        return (kb >= 0) & (worst > -SKIP_BELOW)

    def body(state):
        kb, _ = state
        return kb - 1, step(kb, False)

    lax.while_loop(cond, body, (qi - 1, step(qi, True)))
    o_ref[...] = acc_ref[...].astype(o_ref.dtype)


def _attention(proj, B, S, W):
    T = proj.shape[0]
    n_heads = W // HEAD_DIM
    heads = min(4, n_heads)
    assert n_heads % heads == 0
    hw = heads * HEAD_DIM
    tq = _tile(S, 256)
    nq = S // tq
    q0, k0, v0 = 2 * W // hw, 3 * W // hw, 4 * W // hw
    kern = functools.partial(_attn_kernel, tq=tq, heads=heads)
    return pl.pallas_call(
        kern,
        grid=(B, n_heads // heads, nq),
        in_specs=[pl.BlockSpec((tq, hw), lambda b, g, i: (b * nq + i, q0 + g)),
                  pl.BlockSpec((S, hw), lambda b, g, i: (b, k0 + g)),
                  pl.BlockSpec((S, hw), lambda b, g, i: (b, v0 + g))],
        out_specs=pl.BlockSpec((tq, hw), lambda b, g, i: (b * nq + i, g)),
        out_shape=jax.ShapeDtypeStruct((T, W), BF16),
        scratch_shapes=[pltpu.VMEM((tq, hw), F32),
                        pltpu.VMEM((heads, tq, 1), F32)],
        compiler_params=_params(("parallel", "parallel", "arbitrary")),
        name="stick_breaking",
    )(proj, proj, proj)


def _merge_kernel(a_ref, b_ref, pa_hbm, pb_hbm, ga_ref, gb_ref, o_ref,
                  sa_ref, sb_ref, pab_ref, pbb_ref, sem):
    @pl.when(pl.program_id(1) == 0)
    def _():
        j, n = pl.program_id(0), pl.num_programs(0)
        _load_column_tile(pa_hbm, sa_ref, pab_ref, sem.at[0], j, n)
        _load_column_tile(pb_hbm, sb_ref, pbb_ref, sem.at[1], j, n)

    cw = min(o_ref.shape[1], 256)
    for c in range(o_ref.shape[1] // cw):
        cols = slice(c * cw, (c + 1) * cw)
        ya = jnp.dot(a_ref[...], pab_ref[:, cols], preferred_element_type=F32)
        yb = jnp.dot(b_ref[...], pbb_ref[:, cols], preferred_element_type=F32)
        o_ref[:, cols] = (ga_ref[:, cols].astype(F32) * ya
                          + gb_ref[:, cols].astype(F32) * yb).astype(o_ref.dtype)


def _merge(a, b, proj, w_pa, w_pb, W, D):
    T = a.shape[0]
    tm = _tile(T, 512)
    tn = _tile(W, 1024)
    ga0 = 5 * W // tn
    gb0 = (5 * W + D) // tn
    return pl.pallas_call(
        _merge_kernel,
        grid=(D // tn, T // tm),
        in_specs=[pl.BlockSpec((tm, W), lambda j, i: (i, 0)),
                  pl.BlockSpec((tm, W), lambda j, i: (i, 0)),
                  pl.BlockSpec(memory_space=pl.ANY),
                  pl.BlockSpec(memory_space=pl.ANY),
                  pl.BlockSpec((tm, tn), lambda j, i: (i, ga0 + j)),
                  pl.BlockSpec((tm, tn), lambda j, i: (i, gb0 + j))],
        out_specs=pl.BlockSpec((tm, tn), lambda j, i: (i, j)),
        out_shape=jax.ShapeDtypeStruct((T, D), BF16),
        scratch_shapes=[pltpu.VMEM((W, tn), F32), pltpu.VMEM((W, tn), F32),
                        pltpu.VMEM((W, tn), BF16), pltpu.VMEM((W, tn), BF16),
                        pltpu.SemaphoreType.DMA((2,))],
        compiler_params=_params(("arbitrary", "arbitrary")),
        name="merge",
    )(a, b, w_pa, w_pb, proj, proj)


def _out_kernel(m_ref, w_hbm, x_ref, o_ref, stage_ref, wb_ref, sem):
    @pl.when(pl.program_id(1) == 0)
    def _():
        _load_column_tile(w_hbm, stage_ref, wb_ref, sem, pl.program_id(0), pl.num_programs(0))

    cw = min(o_ref.shape[1], 256)
    for c in range(o_ref.shape[1] // cw):
        cols = slice(c * cw, (c + 1) * cw)
        o_ref[:, cols] = x_ref[:, cols] + jnp.dot(m_ref[...], wb_ref[:, cols],
                                                  preferred_element_type=F32)


def _out_proj(merged, w_out, x):
    T, D = x.shape
    tm = _tile(T, 512)
    tn = _tile(D, 1024)
    return pl.pallas_call(
        _out_kernel,
        grid=(D // tn, T // tm),
        in_specs=[pl.BlockSpec((tm, D), lambda j, i: (i, 0)),
                  pl.BlockSpec(memory_space=pl.ANY),
                  pl.BlockSpec((tm, tn), lambda j, i: (i, j))],
        out_specs=pl.BlockSpec((tm, tn), lambda j, i: (i, j)),
        out_shape=jax.ShapeDtypeStruct((T, D), F32),
        scratch_shapes=[pltpu.VMEM((D, tn), F32), pltpu.VMEM((D, tn), BF16),
                        pltpu.SemaphoreType.DMA(())],
        compiler_params=_params(("arbitrary", "arbitrary")),
        name="out_proj",
    )(merged, w_out, x)


def _route_kernel(x_ref, g_ref, wr_ref, br_ref, xn_ref, e_ref, p_ref, c_ref):
    xn = _rms(x_ref[...], g_ref[...])
    xn_ref[...] = xn
    wr = wr_ref[...]
    wr_hi = wr.astype(BF16)
    wr_lo = (wr - wr_hi.astype(F32)).astype(BF16)
    xn_hi = xn.astype(BF16)
    xn_lo = (xn - xn_hi.astype(F32)).astype(BF16)
    logits = (jnp.dot(xn_hi, wr_hi, preferred_element_type=F32)
              + jnp.dot(xn_lo, wr_hi, preferred_element_type=F32)
              + jnp.dot(xn_hi, wr_lo, preferred_element_type=F32)) + br_ref[...]
    tm = logits.shape[0]
    lane = lax.broadcasted_iota(jnp.int32, logits.shape, 1)
    neg = -jnp.inf

    def top(vals):
        m = jnp.max(vals, axis=1, keepdims=True)
        idx = jnp.min(jnp.where(vals == m, lane, ROUTE_LANES), axis=1, keepdims=True)
        return m, idx

    is_group = lane < N_GROUPS_MOE
    gmax, grp = top(jnp.where(is_group, logits, neg))
    p_grp = 1.0 / jnp.sum(jnp.where(is_group, jnp.exp(logits - gmax), 0.0), axis=1, keepdims=True)
    first = N_GROUPS_MOE + grp * EXPERTS_PER_GROUP
    local = jnp.where((lane >= first) & (lane < first + EXPERTS_PER_GROUP), logits, neg)
    m1, i1 = top(local)
    m2, i2 = top(jnp.where(lane == i1, neg, local))
    e2 = jnp.exp(m2 - m1)
    w1 = p_grp / (1.0 + e2)
    w2 = p_grp * e2 / (1.0 + e2)
    ex1 = i1 - N_GROUPS_MOE
    ex2 = i2 - N_GROUPS_MOE

    hot1 = lane == ex1
    hot2 = lane == ex2
    chosen = (hot1 | hot2).astype(BF16)
    row = lax.broadcasted_iota(jnp.int32, (tm, tm), 0)
    col = lax.broadcasted_iota(jnp.int32, (tm, tm), 1)
    earlier = jnp.dot((col < row).astype(BF16), chosen, preferred_element_type=F32)
    r1 = jnp.sum(jnp.where(hot1, earlier, 0.0), axis=1, keepdims=True).astype(jnp.int32)
    r2 = jnp.sum(jnp.where(hot2, earlier, 0.0), axis=1, keepdims=True).astype(jnp.int32)
    counts = jnp.sum(chosen.astype(F32), axis=0, keepdims=True).astype(jnp.int32)

    e_ref[...] = jnp.where(lane == 0, ex1, jnp.where(lane == 1, ex2,
                           jnp.where(lane == 2, r1, jnp.where(lane == 3, r2, 0))))
    p_ref[...] = jnp.where(lane == 0, w1, jnp.where(lane == 1, w2, 0.0))
    c_ref[...] = jnp.broadcast_to(counts[None], c_ref.shape)


def _route(x1, g_ffn, w_group, b_group, w_router, b_router):
    T, D = x1.shape
    n_logit = w_group.shape[1] + w_router.shape[1]
    assert n_logit <= ROUTE_LANES
    wr = jnp.zeros((D, ROUTE_LANES), F32).at[:, :n_logit].set(jnp.concatenate([w_group, w_router], axis=1))
    br = jnp.zeros((1, ROUTE_LANES), F32).at[0, :n_logit].set(jnp.concatenate([b_group, b_router]))
    tm = _tile(T, 256)
    n_tiles = T // tm
    return tm, pl.pallas_call(
        _route_kernel,
        grid=(n_tiles,),
        in_specs=[pl.BlockSpec((tm, D), lambda i: (i, 0)),
                  pl.BlockSpec((1, D), lambda i: (0, 0)),
                  pl.BlockSpec((D, ROUTE_LANES), lambda i: (0, 0)),
                  pl.BlockSpec((1, ROUTE_LANES), lambda i: (0, 0))],
        out_specs=[pl.BlockSpec((tm, D), lambda i: (i, 0)),
                   pl.BlockSpec((tm, ROUTE_LANES), lambda i: (i, 0)),
                   pl.BlockSpec((tm, ROUTE_LANES), lambda i: (i, 0)),
                   pl.BlockSpec((1, 8, ROUTE_LANES), lambda i: (i, 0, 0))],
        out_shape=[jax.ShapeDtypeStruct((T, D), F32),
                   jax.ShapeDtypeStruct((T, ROUTE_LANES), jnp.int32),
                   jax.ShapeDtypeStruct((T, ROUTE_LANES), F32),
                   jax.ShapeDtypeStruct((n_tiles, 8, ROUTE_LANES), jnp.int32)],
        compiler_params=_params(("parallel",)),
        name="route",
    )(x1, g_ffn.reshape(1, D), wr, br)


def _dispatch_tables(experts, ranks, tile_counts, route_tm, n_exp, blk, n_blocks):
    T = experts.shape[0]
    i32 = jnp.int32
    tile_base = jnp.cumsum(tile_counts, axis=0) - tile_counts
    counts = jnp.sum(tile_counts, axis=0)
    padded = (counts + blk - 1) // blk * blk
    pad_end = jnp.cumsum(padded)
    pad_start = pad_end - padded
    base = jnp.repeat(tile_base + pad_start[None, :], route_tm, axis=0)
    hot = experts[:, :, None] == jnp.arange(n_exp, dtype=i32)[None, None, :]
    pos = (jnp.sum(jnp.where(hot, base[:, None, :], 0), axis=2) + ranks).astype(i32)
    token = jnp.broadcast_to(jnp.arange(T, dtype=i32)[:, None], pos.shape)
    pad_tok = jnp.arange(n_blocks * blk, dtype=i32) % T
    row_tok = pad_tok.at[pos.reshape(-1)].set(token.reshape(-1), unique_indices=True)

    n_used = pad_end[-1] // blk
    bstart = jnp.arange(n_blocks, dtype=i32) * blk
    be = jnp.minimum(jnp.sum((pad_end[None, :] <= bstart[:, None]).astype(i32), axis=1), n_exp - 1)
    be = jnp.where(jnp.arange(n_blocks) < n_used, be, be[n_used - 1])

    blk_id = jnp.arange(n_blocks, dtype=i32)
    blk_used = blk_id < n_used
    blk_first = blk_used & (blk_id == pad_start[be] // blk)
    nxt_blk = blk_id + jnp.maximum(padded[be] // blk, 1)
    blk_has_next = blk_first & (nxt_blk < n_used)
    nxt_blk = jnp.minimum(nxt_blk, n_blocks - 1)
    block_flags = (USED * blk_used + FIRST * blk_first + HAS_NEXT * blk_has_next).astype(i32)
    return pos, row_tok, (be.astype(i32), be[nxt_blk].astype(i32), block_flags)


def _dispatch_kernel(tok_hbm, x_hbm, o_ref, idx_smem, xbuf, idx_sem, row_sem, *, blk, n_blocks):
    i = pl.program_id(0)

    def idx_copy(b):
        return pltpu.make_async_copy(tok_hbm.at[pl.ds(pl.multiple_of(b * blk, blk), blk)],
                                     idx_smem.at[b % 2], idx_sem)

    def row_copy(tok, slot, r):
        return pltpu.make_async_copy(x_hbm.at[pl.ds(tok, 1), :], xbuf.at[slot, pl.ds(r, 1), :],
                                     row_sem.at[slot])

    def issue_rows(b):
        slot = b % 2

        def body(p, c):
            for q in range(2):
                r = 2 * p + q
                row_copy(idx_smem[slot, r], slot, r).start(priority=q)
            return c
        lax.fori_loop(0, blk // 2, body, 0, unroll=8)

    def wait_rows(slot):
        pltpu.make_async_copy(x_hbm.at[pl.ds(0, blk), :], xbuf.at[slot], row_sem.at[slot]).wait()

    @pl.when(i == 0)
    def _():
        idx_copy(0).start()
        idx_copy(0).wait()
        issue_rows(0)
        if n_blocks > 1:
            idx_copy(1).start()

    @pl.when(i + 1 < n_blocks)
    def _():
        idx_copy(i + 1).wait()
        issue_rows(i + 1)

    @pl.when(i + 2 < n_blocks)
    def _():
        idx_copy(i + 2).start()

    slot = i % 2
    wait_rows(slot)
    o_ref[...] = xbuf[slot].astype(o_ref.dtype)


def _dispatch(xn, row_tok, blk, n_blocks):
    T, D = xn.shape
    kern = functools.partial(_dispatch_kernel, blk=blk, n_blocks=n_blocks)
    return pl.pallas_call(
        kern,
        grid=(n_blocks,),
        in_specs=[pl.BlockSpec(memory_space=pl.ANY),
                  pl.BlockSpec(memory_space=pl.ANY)],
        out_specs=pl.BlockSpec((blk, D), lambda i: (i, 0)),
        out_shape=jax.ShapeDtypeStruct((n_blocks * blk, D), BF16),
        scratch_shapes=[pltpu.SMEM((2, blk), jnp.int32),
                        pltpu.VMEM((2, blk, D), F32),
                        pltpu.SemaphoreType.DMA(()),
                        pltpu.SemaphoreType.DMA((2,))],
        compiler_params=_params(("arbitrary",)),
        name="moe_dispatch",
    )(row_tok, xn)


def _moe_up_kernel(e_ref, next_e_ref, flag_ref, x_ref, wg_hbm, wu_hbm, o_ref,
                   sg_ref, su_ref, wgb_ref, wub_ref, sem):
    i = pl.program_id(0)
    flag = flag_ref[i]
    cw = min(o_ref.shape[1], 256)

    def copies(e):
        return (pltpu.make_async_copy(wg_hbm.at[e], sg_ref, sem.at[0]),
                pltpu.make_async_copy(wu_hbm.at[e], su_ref, sem.at[1]))

    def project(convert):
        for c in range(o_ref.shape[1] // cw):
            cols = slice(c * cw, (c + 1) * cw)
            if convert:
                wgb_ref[:, cols] = sg_ref[:, cols].astype(BF16)
                wub_ref[:, cols] = su_ref[:, cols].astype(BF16)
            g = jnp.dot(x_ref[...], wgb_ref[:, cols], preferred_element_type=F32)
            u = jnp.dot(x_ref[...], wub_ref[:, cols], preferred_element_type=F32)
            o_ref[:, cols] = (g * _sigmoid(g) * u).astype(o_ref.dtype)

    @pl.when((flag & FIRST) != 0)
    def _():
        @pl.when(i == 0)
        def _():
            for cp in copies(e_ref[0]):
                cp.start()

        for cp in copies(e_ref[i]):
            cp.wait()
        project(True)

        @pl.when((flag & HAS_NEXT) != 0)
        def _():
            for cp in copies(next_e_ref[i]):
                cp.start()

    @pl.when((flag & (USED | FIRST)) == USED)
    def _():
        project(False)

    @pl.when((flag & USED) == 0)
    def _():
        o_ref[...] = jnp.zeros_like(o_ref)


def _moe_up(xs, block_tables, w_gate, w_up, blk):
    n_rows, D = xs.shape
    hid = w_gate.shape[2]
    grid_spec = pltpu.PrefetchScalarGridSpec(
        num_scalar_prefetch=3,
        grid=(n_rows // blk,),
        in_specs=[pl.BlockSpec((blk, D), lambda i, e, ne, f: (i, 0)),
                  pl.BlockSpec(memory_space=pl.ANY),
                  pl.BlockSpec(memory_space=pl.ANY)],
        out_specs=pl.BlockSpec((blk, hid), lambda i, e, ne, f: (i, 0)),
        scratch_shapes=[pltpu.VMEM((D, hid), F32), pltpu.VMEM((D, hid), F32),
                        pltpu.VMEM((D, hid), BF16), pltpu.VMEM((D, hid), BF16),
                        pltpu.SemaphoreType.DMA((2,))],
    )
    return pl.pallas_call(
        _moe_up_kernel,
        grid_spec=grid_spec,
        out_shape=jax.ShapeDtypeStruct((n_rows, hid), BF16),
        compiler_params=_params(("arbitrary",)),
        name="moe_up",
    )(*block_tables, xs, w_gate, w_up)


def _moe_down_kernel(e_ref, next_e_ref, flag_ref, h_ref, wd_hbm, o_ref, stage_ref, wdb_ref, sem):
    i = pl.program_id(0)
    flag = flag_ref[i]
    cw = min(o_ref.shape[1], 256)

    def copy(e):
        return pltpu.make_async_copy(wd_hbm.at[e], stage_ref, sem)

    def project(convert):
        for c in range(o_ref.shape[1] // cw):
            cols = slice(c * cw, (c + 1) * cw)
            if convert:
                wdb_ref[:, cols] = stage_ref[:, cols].astype(BF16)
            o_ref[:, cols] = jnp.dot(h_ref[...], wdb_ref[:, cols], preferred_element_type=F32)

    @pl.when((flag & FIRST) != 0)
    def _():
        @pl.when(i == 0)
        def _():
            copy(e_ref[0]).start()

        copy(e_ref[i]).wait()
        project(True)

        @pl.when((flag & HAS_NEXT) != 0)
        def _():
            copy(next_e_ref[i]).start()

    @pl.when((flag & (USED | FIRST)) == USED)
    def _():
        project(False)

    @pl.when((flag & USED) == 0)
    def _():
        o_ref[...] = jnp.zeros_like(o_ref)


def _moe_down(hid, block_tables, w_down, blk):
    n_rows, H = hid.shape
    D = w_down.shape[2]
    grid_spec = pltpu.PrefetchScalarGridSpec(
        num_scalar_prefetch=3,
        grid=(n_rows // blk,),
        in_specs=[pl.BlockSpec((blk, H), lambda i, e, ne, f: (i, 0)),
                  pl.BlockSpec(memory_space=pl.ANY)],
        out_specs=pl.BlockSpec((blk, D), lambda i, e, ne, f: (i, 0)),
        scratch_shapes=[pltpu.VMEM((H, D), F32), pltpu.VMEM((H, D), BF16),
                        pltpu.SemaphoreType.DMA(())],
    )
    return pl.pallas_call(
        _moe_down_kernel,
        grid_spec=grid_spec,
        out_shape=jax.ShapeDtypeStruct((n_rows, D), F32),
        compiler_params=_params(("arbitrary",)),
        name="moe_down",
    )(*block_tables, hid, w_down)


def _combine_kernel(pos_hbm, y_hbm, x_ref, w_ref, g_ref, o_ref,
                    idx_smem, ybuf, idx_sem, row_sem, *, tm, n_tiles, final_norm):
    i = pl.program_id(0)
    n_idx = TOP_K * tm

    def idx_copy(t):
        return pltpu.make_async_copy(pos_hbm.at[pl.ds(pl.multiple_of(t * n_idx, n_idx), n_idx)],
                                     idx_smem.at[t % 2], idx_sem)

    def row_copy(src, slot, dst):
        return pltpu.make_async_copy(y_hbm.at[pl.ds(src, 1), :], ybuf.at[slot, pl.ds(dst, 1), :],
                                     row_sem.at[slot])

    def issue_rows(t):
        slot = t % 2

        def body(r, c):
            for k in range(TOP_K):
                row_copy(idx_smem[slot, TOP_K * r + k], slot, k * tm + r).start(priority=k % 2)
            return c
        lax.fori_loop(0, tm, body, 0, unroll=8)

    def wait_rows(slot):
        pltpu.make_async_copy(y_hbm.at[pl.ds(0, n_idx), :], ybuf.at[slot], row_sem.at[slot]).wait()

    @pl.when(i == 0)
    def _():
        idx_copy(0).start()
        idx_copy(0).wait()
        issue_rows(0)
        if n_tiles > 1:
            idx_copy(1).start()

    @pl.when(i + 1 < n_tiles)
    def _():
        idx_copy(i + 1).wait()
        issue_rows(i + 1)

    @pl.when(i + 2 < n_tiles)
    def _():
        idx_copy(i + 2).start()

    slot = i % 2
    wait_rows(slot)
    w = w_ref[...]
    out = x_ref[...] + w[:, 0:1] * ybuf[slot, 0:tm, :] + w[:, 1:2] * ybuf[slot, tm:2 * tm, :]
    if final_norm:
        out = _rms(out, g_ref[...])
    o_ref[...] = out


def _combine(x1, y, pos, w_assign, g_final, final_norm):
    T, D = x1.shape
    tm = _tile(T, 256)
    n_tiles = T // tm
    kern = functools.partial(_combine_kernel, tm=tm, n_tiles=n_tiles, final_norm=final_norm)
    return pl.pallas_call(
        kern,
        grid=(n_tiles,),
        in_specs=[pl.BlockSpec(memory_space=pl.ANY),
                  pl.BlockSpec(memory_space=pl.ANY),
                  pl.BlockSpec((tm, D), lambda i: (i, 0)),
                  pl.BlockSpec((tm, TOP_K), lambda i: (i, 0)),
                  pl.BlockSpec((1, D), lambda i: (0, 0))],
        out_specs=pl.BlockSpec((tm, D), lambda i: (i, 0)),
        out_shape=jax.ShapeDtypeStruct((T, D), F32),
        scratch_shapes=[pltpu.SMEM((2, TOP_K * tm), jnp.int32),
                        pltpu.VMEM((2, TOP_K * tm, D), F32),
                        pltpu.SemaphoreType.DMA(()),
                        pltpu.SemaphoreType.DMA((2,))],
        compiler_params=_params(("arbitrary",)),
        name="moe_combine",
    )(pos.reshape(-1), y, x1, w_assign, g_final.reshape(1, D))


def _layer(x, B, S, p, g_final, final_norm):
    T, D = x.shape
    W = D // 2
    xn = _norm_bf16(x, p["g_mix"])
    proj = _in_proj(xn, p["w_in"], p["b_gate"], W)
    a = _sgu(proj, p["g_sgu"], p["w_sgu"], p["b_sgu"], W)
    b = _attention(proj, B, S, W)
    merged = _merge(a, b, proj, p["w_proj_a"], p["w_proj_b"], W, D)
    x1 = _out_proj(merged, p["w_out"], x)

    route_tm, (xn2, e_out, p_out, tile_counts) = _route(
        x1, p["g_ffn"], p["w_group"], p["b_group"], p["w_router"], p["b_router"])
    n_exp = p["w_gate"].shape[0]
    blk = 256
    n_blocks = T * TOP_K // blk + n_exp
    pos, row_tok, block_tables = _dispatch_tables(
        e_out[:, :TOP_K], e_out[:, TOP_K:2 * TOP_K], tile_counts[:, 0, :n_exp],
        route_tm, n_exp, blk, n_blocks)
    xs = _dispatch(xn2, row_tok, blk, n_blocks)
    hid = _moe_up(xs, block_tables, p["w_gate"], p["w_up"], blk)
    y = _moe_down(hid, block_tables, p["w_down"], blk)
    return _combine(x1, y, pos, p_out[:, :TOP_K], g_final, final_norm)


def kernel(x, g_mix, w_in, g_sgu, w_sgu, b_sgu, b_gate, w_proj_a, w_proj_b, w_out,
           g_ffn, w_group, b_group, w_router, b_router, w_gate, w_up, w_down, g_final):
    B, S, D = x.shape
    stacked = dict(g_mix=g_mix, w_in=w_in, g_sgu=g_sgu, w_sgu=w_sgu, b_sgu=b_sgu, b_gate=b_gate,
                   w_proj_a=w_proj_a, w_proj_b=w_proj_b, w_out=w_out, g_ffn=g_ffn,
                   w_group=w_group, b_group=b_group, w_router=w_router, b_router=b_router,
                   w_gate=w_gate, w_up=w_up, w_down=w_down)
    depth = w_in.shape[0]
    xf = x.reshape(B * S, D)
    for l in range(depth):
        p = {name: val[l] for name, val in stacked.items()}
        xf = _layer(xf, B, S, p, g_final, final_norm=(l == depth - 1))
    return xf.reshape(B, S, D)
```
